```python
import math
import jax, jax.numpy as jnp
from jax import lax
import numpy as np

D_MODEL = 1024
BATCH = 8
SEQ = 2048
DEPTH = 4
DEC_BATCH = 32
DEC_SEQ = 1
PAST_LEN = 16384
PAGE_SIZE = 128

N_MIXERS = 3
FFN_DIM = 2816
MEM_LEN = 256
X_HEADS = 4
X_HEAD_DIM = D_MODEL // X_HEADS
ROPE_THETA = 10000.0
NORM_EPS = 1e-6
NEG_INF = -1e30
Q_BLOCK = 128
PAGES_PER_SWEEP = 16

MLA_HEADS = 8
MLA_NOPE = 128
MLA_ROPE = 64
MLA_V = 128
MLA_KV_LORA = D_MODEL // 4
MLA_Q_LORA = 3 * D_MODEL // 8
MLA_IN = MLA_Q_LORA + MLA_KV_LORA + MLA_ROPE
MLA_ROW = MLA_KV_LORA + MLA_ROPE

DIL_GROUPS = ((128, 1), (512, 4), (2048, 16))
DIL_HEADS = 8
DIL_HEAD_DIM = D_MODEL // DIL_HEADS
DIL_BLOCK = 128

DIFF_HEADS = 8
DIFF_HEAD_DIM = D_MODEL // (2 * DIFF_HEADS)

N_MLA = (DEPTH + 2) // 3
N_DIL = (DEPTH + 1) // 3
N_DIFF = DEPTH // 3

kernel_name = 'hybrid_mla_dilated_diff_decoder_step'


def rms_norm(x, g):
    xf = x.astype(jnp.float32)
    y = xf * lax.rsqrt(jnp.mean(xf * xf, axis=-1, keepdims=True) + NORM_EPS)
    return (y * g.astype(jnp.float32)).astype(x.dtype)


def rope(x, pos):
    dim = x.shape[-1]
    half = dim // 2
    inv = jnp.exp(jnp.arange(half, dtype=jnp.float32) * (-2.0 * math.log(ROPE_THETA) / dim))
    ang = pos.astype(jnp.float32)[:, None] * inv[None, :]
    cos, sin = jnp.cos(ang)[:, None, :], jnp.sin(ang)[:, None, :]
    xf = x.astype(jnp.float32)
    x1, x2 = xf[..., :half], xf[..., half:]
    return jnp.concatenate([x1 * cos - x2 * sin, x2 * cos + x1 * sin], axis=-1).astype(x.dtype)


def swiglu(x, w_gu, w_down):
    g, u = jnp.split(x @ w_gu, 2, axis=-1)
    return (jax.nn.silu(g) * u) @ w_down


def softmax_stats(s, mask, v, eq):
    if mask is not None:
        s = jnp.where(mask, s, NEG_INF)
    m = jnp.max(s, axis=-1)
    p = jnp.exp(s - m[..., None])
    if mask is not None:
        p = jnp.where(mask, p, 0.0)
    return (m, jnp.sum(p, axis=-1), jnp.einsum(eq, p, v))


def merge_stats(a, b):
    m = jnp.maximum(a[0], b[0])
    ca, cb = jnp.exp(a[0] - m), jnp.exp(b[0] - m)
    return (m, a[1] * ca + b[1] * cb, a[2] * ca[..., None] + b[2] * cb[..., None])


def causal_query_blocks(block_fn, qs, seq):
    nb = seq // Q_BLOCK
    xs = tuple(jnp.moveaxis(q.reshape(q.shape[0], nb, Q_BLOCK, *q.shape[2:]), 1, 0) for q in qs)
    out = lax.map(lambda a: block_fn(a[0], *a[1:]), (jnp.arange(nb), *xs))
    out = jnp.moveaxis(out, 0, 1)
    return out.reshape(out.shape[0], seq, *out.shape[3:])


def causal_block_mask(b, seq):
    q_pos = b * Q_BLOCK + jnp.arange(Q_BLOCK)
    return jnp.arange(seq)[None, :] <= q_pos[:, None]


def sweep_pages(page_table, pools, layer, chunk_fn, init):
    db, n_pages = page_table.shape
    per = math.gcd(n_pages, PAGES_PER_SWEEP)
    ids = page_table.reshape(db, n_pages // per, per).transpose(1, 0, 2)

    def step(carry, pid):
        rows = tuple(p[layer, pid].reshape(db, per * PAGE_SIZE, *p.shape[3:]) for p in pools)
        return merge_stats(carry, chunk_fn(*rows)), None

    out, _ = lax.scan(step, init, ids)
    return out


def mla_project(h, w_in, q_norm_g, w_q_up, kv_norm_g, pos):
    b, s, _ = h.shape
    cq, ckv, kr = jnp.split(h @ w_in, [MLA_Q_LORA, MLA_Q_LORA + MLA_KV_LORA], axis=-1)
    q = (rms_norm(cq, q_norm_g) @ w_q_up).reshape(b, s, MLA_HEADS, MLA_NOPE + MLA_ROPE)
    q_nope, q_rope = q[..., :MLA_NOPE], rope(q[..., MLA_NOPE:], pos)
    row = jnp.concatenate([rms_norm(ckv, kv_norm_g), rope(kr[:, :, None, :], pos)[:, :, 0]], axis=-1)
    return q_nope, q_rope, row


def mla_prompt_attend(q_nope, q_rope, row, w_kv_up):
    b, s, _ = row.shape
    c, kr = row[..., :MLA_KV_LORA], row[..., MLA_KV_LORA:]
    kv = (c @ w_kv_up).reshape(b, s, MLA_HEADS, MLA_NOPE + MLA_V)
    k_nope, v = kv[..., :MLA_NOPE], kv[..., MLA_NOPE:]
    scale = (MLA_NOPE + MLA_ROPE) ** -0.5

    def block(bi, qn, qr):
        sc = (jnp.einsum('bqhd,bkhd->bhqk', qn, k_nope)
              + jnp.einsum('bqhr,bkr->bhqk', qr, kr)).astype(jnp.float32) * scale
        p = jax.nn.softmax(jnp.where(causal_block_mask(bi, s), sc, NEG_INF), axis=-1)
        return jnp.einsum('bhqk,bkhd->bqhd', p.astype(v.dtype), v)

    return causal_query_blocks(block, (q_nope, q_rope), s)


def mla_sample_attend(q_nope, q_rope, row, w_kv_up, pool, layer, page_table):
    w = w_kv_up.reshape(MLA_KV_LORA, MLA_HEADS, MLA_NOPE + MLA_V)
    w_uk, w_uv = w[..., :MLA_NOPE], w[..., MLA_NOPE:]
    q_lat = jnp.einsum('bthd,chd->bthc', q_nope, w_uk)
    scale = (MLA_NOPE + MLA_ROPE) ** -0.5
    t = row.shape[1]

    def stats(rows, mask):
        c, kr = rows[..., :MLA_KV_LORA], rows[..., MLA_KV_LORA:]
        sc = (jnp.einsum('bthc,bnc->bthn', q_lat, c)
              + jnp.einsum('bthr,bnr->bthn', q_rope, kr)).astype(jnp.float32) * scale
        return softmax_stats(sc, mask, c, 'bthn,bnc->bthc')

    causal = jnp.tril(jnp.ones((t, t), bool))[:, None, :]
    m, l, o = sweep_pages(page_table, (pool,), layer, lambda rows: stats(rows, None), stats(row, causal))
    o_lat = (o / l[..., None]).astype(row.dtype)
    return jnp.einsum('bthc,chd->bthd', o_lat, w_uv)


def dil_qkv(h, w_in, pos):
    b, s, _ = h.shape
    g = len(DIL_GROUPS)
    qkv = (h @ w_in).reshape(b, s, g, 3, DIL_HEADS, DIL_HEAD_DIM)
    qk = rope(qkv[:, :, :, :2].reshape(b, s, g * 2 * DIL_HEADS, DIL_HEAD_DIM), pos)
    qk = qk.reshape(b, s, g, 2, DIL_HEADS, DIL_HEAD_DIM)
    return qk[:, :, :, 0], qk[:, :, :, 1], qkv[:, :, :, 2]


def banded_attention(q, k, v, span):
    n, L, h, hd = q.shape
    nb = -(-L // DIL_BLOCK)
    lp = nb * DIL_BLOCK
    pad = lambda a, front: jnp.pad(a, ((0, 0), (front, lp - L), (0, 0), (0, 0)))
    qb = pad(q, 0).reshape(n, nb, DIL_BLOCK, h, hd)

    def windows(a):
        ap = pad(a, DIL_BLOCK)
        return jnp.concatenate([ap[:, :lp].reshape(n, nb, DIL_BLOCK, h, hd),
                                ap[:, DIL_BLOCK:].reshape(n, nb, DIL_BLOCK, h, hd)], axis=2)

    kb, vb = windows(k), windows(v)
    sc = jnp.einsum('nbqhd,nbkhd->nbhqk', qb, kb).astype(jnp.float32) * hd ** -0.5
    qi = jnp.arange(DIL_BLOCK)[:, None] + DIL_BLOCK
    kj = jnp.arange(2 * DIL_BLOCK)[None, :]
    dist = qi - kj
    kpos = jnp.arange(nb)[:, None, None] * DIL_BLOCK + kj[None] - DIL_BLOCK
    mask = (((dist >= 0) & (dist <= span))[None] & (kpos >= 0))[None, :, None]
    m, l, o = softmax_stats(sc, mask, vb, 'nbhqk,nbkhd->nbhqd')
    o = (o / l[..., None]).transpose(0, 1, 3, 2, 4).reshape(n, lp, h, hd)[:, :L]
    lse = (m + jnp.log(l)).transpose(0, 1, 3, 2).reshape(n, lp, h)[:, :L]
    return o, lse


def combine_groups(outs, lses):
    alpha = jax.nn.softmax(jnp.stack(lses, axis=0), axis=0)
    return jnp.sum(alpha[..., None] * jnp.stack(outs, axis=0), axis=0)


def dil_prompt_attend(q, k, v):
    b, s = q.shape[:2]
    outs, lses = [], []
    for g, (w, d) in enumerate(DIL_GROUPS):
        to_res = lambda a: a[:, :, g].reshape(b, s // d, d, DIL_HEADS, DIL_HEAD_DIM).transpose(
            0, 2, 1, 3, 4).reshape(b * d, s // d, DIL_HEADS, DIL_HEAD_DIM)
        o, lse = banded_attention(to_res(q), to_res(k), to_res(v), w // d)
        outs.append(o.reshape(b, d, s // d, DIL_HEADS, DIL_HEAD_DIM).transpose(0, 2, 1, 3, 4)
                    .reshape(b, s, DIL_HEADS, DIL_HEAD_DIM))
        lses.append(lse.reshape(b, d, s // d, DIL_HEADS).transpose(0, 2, 1, 3).reshape(b, s, DIL_HEADS))
    return combine_groups(outs, lses)


def dil_sample_attend(q, k, v, bufs, layer):
    t = q.shape[1]
    outs, lses = [], []
    for g, (w, d) in enumerate(DIL_GROUPS):
        buf = bufs[g][layer]
        wb = buf.shape[1]
        full = jnp.concatenate([buf, jnp.stack([k[:, :, g], v[:, :, g]], axis=2)], axis=1)
        idx = wb + jnp.arange(t)[:, None] - d * jnp.arange(w // d + 1)[None, :]
        valid = idx >= 0
        rows = full[:, jnp.maximum(idx, 0)]
        sc = jnp.einsum('bthd,btnhd->bthn', q[:, :, g], rows[:, :, :, 0]).astype(jnp.float32) * DIL_HEAD_DIM ** -0.5
        m, l, o = softmax_stats(sc, valid[None, :, None, :], rows[:, :, :, 1], 'bthn,btnhd->bthd')
        outs.append(o / l[..., None])
        lses.append(m + jnp.log(l))
    return combine_groups(outs, lses)


def diff_qkv(h, w_in, pos):
    b, s, _ = h.shape
    qk_w = 2 * DIFF_HEADS * DIFF_HEAD_DIM
    q, k, v = jnp.split(h @ w_in, [qk_w, 2 * qk_w], axis=-1)
    q = rope(q.reshape(b, s, 2 * DIFF_HEADS, DIFF_HEAD_DIM), pos)
    k = rope(k.reshape(b, s, 2 * DIFF_HEADS, DIFF_HEAD_DIM), pos)
    return q, k, v.reshape(b, s, DIFF_HEADS, 2 * DIFF_HEAD_DIM)


def diff_lambda_value(lam_p, lam_init):
    lp = lam_p.astype(jnp.float32)
    return jnp.exp(jnp.sum(lp[0] * lp[1])) - jnp.exp(jnp.sum(lp[2] * lp[3])) + lam_init


def diff_prompt_attend(q, k, v, lam):
    b, s = q.shape[:2]
    qh = q.reshape(b, s, DIFF_HEADS, 2, DIFF_HEAD_DIM)
    kh = k.reshape(b, s, DIFF_HEADS, 2, DIFF_HEAD_DIM)

    def block(bi, qb):
        sc = jnp.einsum('bqhcd,bkhcd->bhcqk', qb, kh).astype(jnp.float32) * DIFF_HEAD_DIM ** -0.5
        p = jax.nn.softmax(jnp.where(causal_block_mask(bi, s), sc, NEG_INF), axis=-1)
        pd = p[:, :, 0] - lam * p[:, :, 1]
        return jnp.einsum('bhqk,bkhe->bqhe', pd.astype(v.dtype), v)

    return causal_query_blocks(block, (qh,), s)


def diff_sample_attend(q, k, v, lam, pool_k, pool_v, layer, page_table):
    db, t = q.shape[:2]
    qh = q.reshape(db, t, DIFF_HEADS, 2, DIFF_HEAD_DIM)

    def stats(kr, vr, mask):
        kh = kr.reshape(db, kr.shape[1], DIFF_HEADS, 2, DIFF_HEAD_DIM)
        sc = jnp.einsum('bthcd,bnhcd->bthcn', qh, kh).astype(jnp.float32) * DIFF_HEAD_DIM ** -0.5
        return softmax_stats(sc, mask, vr, 'bthcn,bnhe->bthce')

    causal = jnp.tril(jnp.ones((t, t), bool))[:, None, None, :]
    m, l, o = sweep_pages(page_table, (pool_k, pool_v), layer,
                          lambda kr, vr: stats(kr, vr, None), stats(k, v, causal))
    o = o / l[..., None]
    return o[:, :, :, 0] - lam * o[:, :, :, 1]


def diff_head_norm(o, g, lam_init):
    return rms_norm(o.astype(jnp.float32), g) * (1.0 - lam_init)


def memory_kv(mem, g, w_kv):
    b, m, _ = mem.shape
    return (rms_norm(mem, g) @ w_kv).reshape(b, m, 2, X_HEADS, X_HEAD_DIM)


def cross_attend(h, mkv, w_q, w_o):
    b, s, _ = h.shape
    q = (h @ w_q).reshape(b, s, X_HEADS, X_HEAD_DIM)
    sc = jnp.einsum('bshd,bmhd->bhsm', q, mkv[:, :, 0]).astype(jnp.float32) * X_HEAD_DIM ** -0.5
    p = jax.nn.softmax(sc, axis=-1)
    o = jnp.einsum('bhsm,bmhd->bshd', p.astype(h.dtype), mkv[:, :, 1])
    return o.reshape(b, s, X_HEADS * X_HEAD_DIM) @ w_o


def setup_inputs(seed: int = 0) -> dict:
    key = jax.random.key(seed)
    ks = iter(jax.random.split(key, 64))
    f32 = jnp.float32
    nrm = lambda shape: jax.random.normal(next(ks), shape, f32)
    w = lambda shape, fan_in: jax.random.normal(next(ks), shape, f32) * fan_in ** -0.5
    gain = lambda shape: 1.0 + 0.02 * jax.random.normal(next(ks), shape, f32)
    n_pages = PAST_LEN // PAGE_SIZE
    used = DEC_BATCH * n_pages
    n_pool = used + max(1, used // 4)
    page_table = jax.random.permutation(next(ks), n_pool)[:used].reshape(DEC_BATCH, n_pages).astype(jnp.int32)
    g_count = len(DIL_GROUPS)
    return {
        'x_prompt': nrm((BATCH, SEQ, D_MODEL)),
        'x_sample': nrm((DEC_BATCH, DEC_SEQ, D_MODEL)),
        'cache_mla': nrm((N_MLA, n_pool, PAGE_SIZE, MLA_ROW)),
        'cache_dil_g0': nrm((N_DIL, DEC_BATCH, min(DIL_GROUPS[0][0], PAST_LEN), 2, DIL_HEADS, DIL_HEAD_DIM)),
        'cache_dil_g1': nrm((N_DIL, DEC_BATCH, min(DIL_GROUPS[1][0], PAST_LEN), 2, DIL_HEADS, DIL_HEAD_DIM)),
        'cache_dil_g2': nrm((N_DIL, DEC_BATCH, min(DIL_GROUPS[2][0], PAST_LEN), 2, DIL_HEADS, DIL_HEAD_DIM)),
        'cache_diff_k': nrm((N_DIFF, n_pool, PAGE_SIZE, 2 * DIFF_HEADS, DIFF_HEAD_DIM)),
        'cache_diff_v': nrm((N_DIFF, n_pool, PAGE_SIZE, DIFF_HEADS, 2 * DIFF_HEAD_DIM)),
        'cache_mem_kv': nrm((DEPTH, DEC_BATCH, MEM_LEN, 2, X_HEADS, X_HEAD_DIM)),
        'page_table': page_table,
        'mem_prompt': nrm((BATCH, MEM_LEN, D_MODEL)),
        'norm_g': gain((DEPTH, 4, D_MODEL)),
        'mem_norm_g': gain((DEPTH, D_MODEL)),
        'final_norm_g': gain((D_MODEL,)),
        'ffn_w_gu': w((DEPTH, 2, D_MODEL, 2 * FFN_DIM), D_MODEL),
        'ffn_w_down': w((DEPTH, 2, FFN_DIM, D_MODEL), FFN_DIM),
        'x_wq': w((DEPTH, D_MODEL, X_HEADS * X_HEAD_DIM), D_MODEL),
        'x_wkv': w((DEPTH, D_MODEL, 2 * X_HEADS * X_HEAD_DIM), D_MODEL),
        'x_wo': w((DEPTH, X_HEADS * X_HEAD_DIM, D_MODEL), X_HEADS * X_HEAD_DIM),
        'mla_w_in': w((N_MLA, D_MODEL, MLA_IN), D_MODEL),
        'mla_q_norm_g': gain((N_MLA, MLA_Q_LORA)),
        'mla_w_q_up': w((N_MLA, MLA_Q_LORA, MLA_HEADS * (MLA_NOPE + MLA_ROPE)), MLA_Q_LORA),
        'mla_kv_norm_g': gain((N_MLA, MLA_KV_LORA)),
        'mla_w_kv_up': w((N_MLA, MLA_KV_LORA, MLA_HEADS * (MLA_NOPE + MLA_V)), MLA_KV_LORA),
        'mla_w_out': w((N_MLA, MLA_HEADS * MLA_V, D_MODEL), MLA_HEADS * MLA_V),
        'dil_w_in': w((N_DIL, D_MODEL, g_count * 3 * DIL_HEADS * DIL_HEAD_DIM), D_MODEL),
        'dil_w_out': w((N_DIL, DIL_HEADS * DIL_HEAD_DIM, D_MODEL), DIL_HEADS * DIL_HEAD_DIM),
        'diff_w_in': w((N_DIFF, D_MODEL, 6 * DIFF_HEADS * DIFF_HEAD_DIM), D_MODEL),
        'diff_lambda': 0.1 * nrm((N_DIFF, 4, DIFF_HEAD_DIM)),
        'diff_subln_g': gain((N_DIFF, 2 * DIFF_HEAD_DIM)),
        'diff_w_out': w((N_DIFF, 2 * DIFF_HEADS * DIFF_HEAD_DIM, D_MODEL), 2 * DIFF_HEADS * DIFF_HEAD_DIM),
    }


def reference(x_prompt, x_sample, cache_mla, cache_dil_g0, cache_dil_g1, cache_dil_g2,
              cache_diff_k, cache_diff_v, cache_mem_kv, page_table, mem_prompt,
              norm_g, mem_norm_g, final_norm_g, ffn_w_gu, ffn_w_down, x_wq, x_wkv, x_wo,
              mla_w_in, mla_q_norm_g, mla_w_q_up, mla_kv_norm_g, mla_w_kv_up, mla_w_out,
              dil_w_in, dil_w_out, diff_w_in, diff_lambda, diff_subln_g, diff_w_out):
    bp, sp, _ = x_prompt.shape
    bs, ts, _ = x_sample.shape
    past_len = page_table.shape[1] * PAGE_SIZE
    pos_p = jnp.arange(sp, dtype=jnp.int32)
    pos_s = past_len + jnp.arange(ts, dtype=jnp.int32)
    dil_bufs = (cache_dil_g0, cache_dil_g1, cache_dil_g2)
    n_pg = sp // PAGE_SIZE
    mla_p, mla_s, diff_kp, diff_ks, diff_vp, diff_vs, mem_p = [], [], [], [], [], [], []
    dil_p = [[] for _ in DIL_GROUPS]
    dil_s = [[] for _ in DIL_GROUPS]
    xp, xs = x_prompt, x_sample
    for i in range(DEPTH):
        kind, j = i % N_MIXERS, i // N_MIXERS
        xp = xp + 0.5 * swiglu(rms_norm(xp, norm_g[i, 0]), ffn_w_gu[i, 0], ffn_w_down[i, 0])
        xs = xs + 0.5 * swiglu(rms_norm(xs, norm_g[i, 0]), ffn_w_gu[i, 0], ffn_w_down[i, 0])
        hp, hs = rms_norm(xp, norm_g[i, 1]), rms_norm(xs, norm_g[i, 1])
        if kind == 0:
            qn, qr, row = mla_project(hp, mla_w_in[j], mla_q_norm_g[j], mla_w_q_up[j], mla_kv_norm_g[j], pos_p)
            o_p = mla_prompt_attend(qn, qr, row, mla_w_kv_up[j])
            mla_p.append(row.reshape(bp, n_pg, PAGE_SIZE, MLA_ROW))
            qn, qr, row = mla_project(hs, mla_w_in[j], mla_q_norm_g[j], mla_w_q_up[j], mla_kv_norm_g[j], pos_s)
            o_s = mla_sample_attend(qn, qr, row, mla_w_kv_up[j], cache_mla, j, page_table)
            mla_s.append(row)
            w_out = mla_w_out[j]
        elif kind == 1:
            q, k, v = dil_qkv(hp, dil_w_in[j], pos_p)
            o_p = dil_prompt_attend(q, k, v)
            for g, (w, _) in enumerate(DIL_GROUPS):
                wl = min(w, sp)
                dil_p[g].append(jnp.stack([k[:, sp - wl:, g], v[:, sp - wl:, g]], axis=2))
            q, k, v = dil_qkv(hs, dil_w_in[j], pos_s)
            o_s = dil_sample_attend(q, k, v, dil_bufs, j)
            for g in range(len(DIL_GROUPS)):
                dil_s[g].append(jnp.stack([k[:, :, g], v[:, :, g]], axis=2))
            w_out = dil_w_out[j]
        else:
            lam_init = 0.8 - 0.6 * math.exp(-0.3 * i)
            lam = diff_lambda_value(diff_lambda[j], lam_init)
            q, k, v = diff_qkv(hp, diff_w_in[j], pos_p)
            o_p = diff_head_norm(diff_prompt_attend(q, k, v, lam), diff_subln_g[j], lam_init)
            diff_kp.append(k.reshape(bp, n_pg, PAGE_SIZE, 2 * DIFF_HEADS, DIFF_HEAD_DIM))
            diff_vp.append(v.reshape(bp, n_pg, PAGE_SIZE, DIFF_HEADS, 2 * DIFF_HEAD_DIM))
            q, k, v = diff_qkv(hs, diff_w_in[j], pos_s)
            o_s = diff_head_norm(diff_sample_attend(q, k, v, lam, cache_diff_k, cache_diff_v, j, page_table),
                                 diff_subln_g[j], lam_init)
            diff_ks.append(k)
            diff_vs.append(v)
            w_out = diff_w_out[j]
        xp = xp + o_p.reshape(bp, sp, -1).astype(xp.dtype) @ w_out
        xs = xs + o_s.reshape(bs, ts, -1).astype(xs.dtype) @ w_out
        mkv_p = memory_kv(mem_prompt, mem_norm_g[i], x_wkv[i])
        mem_p.append(mkv_p)
        xp = xp + cross_attend(rms_norm(xp, norm_g[i, 2]), mkv_p, x_wq[i], x_wo[i])
        xs = xs + cross_attend(rms_norm(xs, norm_g[i, 2]), cache_mem_kv[i], x_wq[i], x_wo[i])
        xp = xp + 0.5 * swiglu(rms_norm(xp, norm_g[i, 3]), ffn_w_gu[i, 1], ffn_w_down[i, 1])
        xs = xs + 0.5 * swiglu(rms_norm(xs, norm_g[i, 3]), ffn_w_gu[i, 1], ffn_w_down[i, 1])
    return (rms_norm(xp, final_norm_g), rms_norm(xs, final_norm_g),
            jnp.stack(mla_p), jnp.stack(mla_s),
            jnp.stack(dil_p[0]), jnp.stack(dil_s[0]),
            jnp.stack(dil_p[1]), jnp.stack(dil_s[1]),
            jnp.stack(dil_p[2]), jnp.stack(dil_s[2]),
            jnp.stack(diff_kp), jnp.stack(diff_ks),
            jnp.stack(diff_vp), jnp.stack(diff_vs),
            jnp.stack(mem_p))
```

```python
import functools
import math

import jax
import jax.numpy as jnp
from jax import lax
from jax.experimental import pallas as pl
from jax.experimental.pallas import tpu as pltpu

F32 = jnp.float32
BF16 = jnp.bfloat16

NORM_EPS = 1e-6
NEG_INF = -1e30
ROPE_THETA = 10000.0
PAGE_SIZE = 128
N_MIXERS = 3

MLA_HEADS = 8
MLA_NOPE = 128
MLA_ROPE = 64
MLA_V = 128
MLA_SLOT = 256

DIL_GROUPS = ((128, 1), (512, 4), (2048, 16))
DIL_HEADS = 8
DIL_BLOCK = 128

DIFF_HEADS = 8
X_HEADS = 4

LANES = 128
QROWS = 16
VMEM_LIMIT = 48 * 1024 * 1024


def _cparams(sem, vmem=VMEM_LIMIT):
    return pltpu.CompilerParams(dimension_semantics=sem, vmem_limit_bytes=vmem)


def _rms(x, g):
    ms = jnp.mean(x * x, axis=-1, keepdims=True)
    return x * lax.rsqrt(ms + NORM_EPS) * g


def _nt_dot(a, b):
    return lax.dot_general(a, b, (((1,), (1,)), ((), ())), preferred_element_type=F32)


def _rope_tables(pos, dim):
    half = dim // 2
    inv = jnp.exp(jnp.arange(half, dtype=F32) * (-2.0 * math.log(ROPE_THETA) / dim))
    ang = pos.astype(F32)[:, None] * inv[None, :]
    cos, sin = jnp.cos(ang), jnp.sin(ang)
    reps = LANES // dim
    cos_t = jnp.tile(jnp.concatenate([cos, cos], axis=-1), (1, reps))
    sin_t = jnp.tile(jnp.concatenate([-sin, sin], axis=-1), (1, reps))
    return cos_t, sin_t


def _apply_rope(chunk, cos, sin, dim):
    half = dim // 2
    if dim == LANES:
        partner = pltpu.roll(chunk, half, 1)
    else:
        lane = lax.broadcasted_iota(jnp.int32, chunk.shape, 1)
        first = (lane & (dim - 1)) < half
        partner = jnp.where(first, pltpu.roll(chunk, LANES - half, 1), pltpu.roll(chunk, half, 1))
    return chunk * cos + partner * sin


def _ffn_kernel(x_ref, g_ref, wg_ref, wu_ref, wd_ref, o_ref, h_ref, acc_ref, *, nf):
    f = pl.program_id(1)

    @pl.when(f == 0)
    def _():
        h_ref[...] = _rms(x_ref[...], g_ref[...]).astype(BF16)
        acc_ref[...] = jnp.zeros_like(acc_ref)

    h = h_ref[...]
    g = jnp.dot(h, wg_ref[...], preferred_element_type=F32)
    u = jnp.dot(h, wu_ref[...], preferred_element_type=F32)
    a = (g * jax.nn.sigmoid(g) * u).astype(BF16)
    acc_ref[...] += jnp.dot(a, wd_ref[...], preferred_element_type=F32)

    @pl.when(f == nf - 1)
    def _():
        o_ref[...] = x_ref[...] + 0.5 * acc_ref[...]


def _ffn_chunk(f_dim):
    cands = [c for c in range(LANES, f_dim + 1, LANES) if f_dim % c == 0 and c <= 1536]
    return max(cands)


def _ffn(x, g_arr, gidx, wgu, wd, layer, which, tm):
    m, d = x.shape
    f_dim = wd.shape[2]
    tf = _ffn_chunk(f_dim)
    nf = f_dim // tf
    return pl.pallas_call(
        functools.partial(_ffn_kernel, nf=nf),
        grid=(m // tm, nf),
        in_specs=[
            pl.BlockSpec((tm, d), lambda i, f: (i, 0)),
            pl.BlockSpec((None, 1, d), lambda i, f: (gidx, 0, 0)),
            pl.BlockSpec((None, None, d, tf), lambda i, f: (layer, which, 0, f)),
            pl.BlockSpec((None, None, d, tf), lambda i, f: (layer, which, 0, nf + f)),
            pl.BlockSpec((None, None, tf, d), lambda i, f: (layer, which, f, 0)),
        ],
        out_specs=pl.BlockSpec((tm, d), lambda i, f: (i, 0)),
        out_shape=jax.ShapeDtypeStruct((m, d), F32),
        scratch_shapes=[pltpu.VMEM((tm, d), BF16), pltpu.VMEM((tm, d), F32)],
        compiler_params=_cparams(("parallel", "arbitrary")),
        name="ffn",
    )(x, g_arr, wgu, wgu, wd)


def _proj_kernel(*refs, cfg, n_out, has_rope, tn):
    x_ref, g_ref, w_ref = refs[:3]
    k = 3
    if has_rope:
        cos_ref, sin_ref = refs[3:5]
        k = 5
    outs = refs[k:k + n_out]
    h_ref = refs[k + n_out]
    j = pl.program_id(1)

    @pl.when(j == 0)
    def _():
        h_ref[...] = _rms(x_ref[...], g_ref[...]).astype(BF16)

    acc = jnp.dot(h_ref[...], w_ref[...], preferred_element_type=F32)
    for jj, (oi, rdim, scale) in enumerate(cfg):

        @pl.when(j == jj)
        def _(oi=oi, rdim=rdim, scale=scale):
            o = outs[oi]
            if rdim:
                cos, sin = cos_ref[...], sin_ref[...]
                for c in range(tn // LANES):
                    ch = _apply_rope(acc[:, c * LANES:(c + 1) * LANES], cos, sin, rdim)
                    if scale != 1.0:
                        ch = ch * scale
                    o[:, c * LANES:(c + 1) * LANES] = ch.astype(o.dtype)
            else:
                a = acc if scale == 1.0 else acc * scale
                o[...] = a.astype(o.dtype)


def _proj(x, g_arr, gidx, w, wlayer, col0, cfg, out_defs, tm, tn, rope=None):
    m, kdim = x.shape
    nj = len(cfg)
    in_specs = [
        pl.BlockSpec((tm, kdim), lambda i, j: (i, 0)),
        pl.BlockSpec((None, 1, kdim), lambda i, j: (gidx, 0, 0)),
        pl.BlockSpec((None, kdim, tn), lambda i, j: (wlayer, 0, col0 + j)),
    ]
    args = [x, g_arr, w]
    if rope is not None:
        cos, sin, nrb = rope
        in_specs += [pl.BlockSpec((tm, LANES), lambda i, j: (i % nrb, 0))] * 2
        args += [cos, sin]
    out_specs, out_shape = [], []
    for cols, dtype, cb in out_defs:
        out_specs.append(pl.BlockSpec((tm, tn), lambda i, j, cb=cb: (i, cb(j))))
        out_shape.append(jax.ShapeDtypeStruct((m, cols), dtype))
    return pl.pallas_call(
        functools.partial(_proj_kernel, cfg=tuple(cfg), n_out=len(out_defs), has_rope=rope is not None, tn=tn),
        grid=(m // tm, nj),
        in_specs=in_specs,
        out_specs=out_specs,
        out_shape=out_shape,
        scratch_shapes=[pltpu.VMEM((tm, kdim), BF16)],
        compiler_params=_cparams(("parallel", "arbitrary")),
        name="proj",
    )(*args)


def _mla_proj_kernel(x_ref, g_ref, win_ref, qg_ref, wq_ref, kvg_ref, wkv_ref, cos_ref, sin_ref,
                     q_ref, row_ref, krp_ref, kv_ref, *, q_lora, kv_lora, scale):
    h = _rms(x_ref[...], g_ref[...]).astype(BF16)
    p = jnp.dot(h, win_ref[...], preferred_element_type=F32)
    cq = p[:, :q_lora]
    ckv = p[:, q_lora:q_lora + kv_lora]
    kr2 = p[:, q_lora + kv_lora:]
    cos, sin = cos_ref[...], sin_ref[...]

    qn = _rms(cq, qg_ref[...]).astype(BF16)
    q = jnp.dot(qn, wq_ref[...], preferred_element_type=F32)
    for c in range(q.shape[1] // LANES):
        ch = q[:, c * LANES:(c + 1) * LANES]
        if c % 2 == 1:
            ch = _apply_rope(ch, cos, sin, MLA_ROPE)
        q_ref[:, c * LANES:(c + 1) * LANES] = (ch * scale).astype(q_ref.dtype)

    c_n = _rms(ckv, kvg_ref[...])
    kr_rot = _apply_rope(kr2, cos, sin, MLA_ROPE)
    row_ref[:, :kv_lora] = c_n
    row_ref[:, kv_lora:] = kr_rot[:, :MLA_ROPE]
    lane = lax.broadcasted_iota(jnp.int32, kr_rot.shape, 1)
    krp_ref[...] = jnp.where(lane < MLA_ROPE, kr_rot, 0.0).astype(krp_ref.dtype)
    kv_ref[...] = jnp.dot(c_n.astype(BF16), wkv_ref[...], preferred_element_type=F32).astype(kv_ref.dtype)


def _mla_proj(x, g_arr, gidx, win, qg, wq, kvg, wkv, layer, cos, sin, nrb, tm):
    m, d = x.shape
    q_lora, kv_lora = qg.shape[-1], kvg.shape[-1]
    n_in = win.shape[-1]
    nq = wq.shape[-1]
    nkv = wkv.shape[-1]
    scale = (MLA_NOPE + MLA_ROPE) ** -0.5
    return pl.pallas_call(
        functools.partial(_mla_proj_kernel, q_lora=q_lora, kv_lora=kv_lora, scale=scale),
        grid=(m // tm,),
        in_specs=[
            pl.BlockSpec((tm, d), lambda i: (i, 0)),
            pl.BlockSpec((None, 1, d), lambda i: (gidx, 0, 0)),
            pl.BlockSpec((None, d, n_in), lambda i: (layer, 0, 0)),
            pl.BlockSpec((None, 1, q_lora), lambda i: (layer, 0, 0)),
            pl.BlockSpec((None, q_lora, nq), lambda i: (layer, 0, 0)),
            pl.BlockSpec((None, 1, kv_lora), lambda i: (layer, 0, 0)),
            pl.BlockSpec((None, kv_lora, nkv), lambda i: (layer, 0, 0)),
            pl.BlockSpec((tm, LANES), lambda i: (i % nrb, 0)),
            pl.BlockSpec((tm, LANES), lambda i: (i % nrb, 0)),
        ],
        out_specs=[
            pl.BlockSpec((tm, nq), lambda i: (i, 0)),
            pl.BlockSpec((tm, kv_lora + MLA_ROPE), lambda i: (i, 0)),
            pl.BlockSpec((tm, LANES), lambda i: (i, 0)),
            pl.BlockSpec((tm, nkv), lambda i: (i, 0)),
        ],
        out_shape=[
            jax.ShapeDtypeStruct((m, nq), BF16),
            jax.ShapeDtypeStruct((m, kv_lora + MLA_ROPE), F32),
            jax.ShapeDtypeStruct((m, LANES), BF16),
            jax.ShapeDtypeStruct((m, nkv), BF16),
        ],
        compiler_params=_cparams(("parallel",)),
        name="mla_proj",
    )(x, g_arr, win, qg, wq, kvg, wkv, cos, sin)


def _flash_update(s, v, m_ref, l_ref, acc_ref, idx, lo, hi):
    m_prev = m_ref[idx]
    m_new = jnp.maximum(m_prev, jnp.max(s, axis=-1, keepdims=True))
    alpha = jnp.exp(m_prev - m_new)
    p = jnp.exp(s - m_new)
    l_ref[idx] = alpha * l_ref[idx] + jnp.sum(p, axis=-1, keepdims=True)
    acc_ref[:, lo:hi] = alpha * acc_ref[:, lo:hi] + jnp.dot(p.astype(BF16), v, preferred_element_type=F32)
    m_ref[idx] = m_new


def _mla_attn_kernel(q_ref, kn_ref, v_ref, krp_ref, o_ref, m_ref, l_ref, acc_ref, *, heads, nk):
    qi, ki = pl.program_id(1), pl.program_id(2)

    @pl.when(ki == 0)
    def _():
        m_ref[...] = jnp.full_like(m_ref, NEG_INF)
        l_ref[...] = jnp.zeros_like(l_ref)
        acc_ref[...] = jnp.zeros_like(acc_ref)

    @pl.when(ki <= qi)
    def _():
        tq, tk = q_ref.shape[0], kn_ref.shape[0]
        row = lax.broadcasted_iota(jnp.int32, (tq, tk), 0)
        col = lax.broadcasted_iota(jnp.int32, (tq, tk), 1)
        mask = (row >= col) | (ki < qi)
        krp = krp_ref[...]
        for h in range(heads):
            qh = q_ref[:, h * MLA_SLOT:(h + 1) * MLA_SLOT]
            kh = jnp.concatenate([kn_ref[:, h * MLA_NOPE:(h + 1) * MLA_NOPE], krp], axis=-1)
            s = jnp.where(mask, _nt_dot(qh, kh), NEG_INF)
            _flash_update(s, v_ref[:, h * MLA_V:(h + 1) * MLA_V], m_ref, l_ref, acc_ref,
                          h, h * MLA_V, (h + 1) * MLA_V)

    @pl.when(ki == nk - 1)
    def _():
        for h in range(heads):
            lo, hi = h * MLA_V, (h + 1) * MLA_V
            o_ref[:, lo:hi] = (acc_ref[:, lo:hi] / l_ref[h]).astype(o_ref.dtype)


def _mla_attn(q, kv, krp, batch, seq, tq):
    m = q.shape[0]
    nq = seq // tq
    hv = MLA_HEADS * MLA_V
    kblk = lambda b, qi, ki: b * nq + jnp.minimum(ki, qi)
    return pl.pallas_call(
        functools.partial(_mla_attn_kernel, heads=MLA_HEADS, nk=nq),
        grid=(batch, nq, nq),
        in_specs=[
            pl.BlockSpec((tq, MLA_HEADS * MLA_SLOT), lambda b, qi, ki: (b * nq + qi, 0)),
            pl.BlockSpec((tq, MLA_HEADS * MLA_NOPE), lambda b, qi, ki: (kblk(b, qi, ki), 0)),
            pl.BlockSpec((tq, hv), lambda b, qi, ki: (kblk(b, qi, ki), 1)),
            pl.BlockSpec((tq, LANES), lambda b, qi, ki: (kblk(b, qi, ki), 0)),
        ],
        out_specs=pl.BlockSpec((tq, hv), lambda b, qi, ki: (b * nq + qi, 0)),
        out_shape=jax.ShapeDtypeStruct((m, hv), BF16),
        scratch_shapes=[
            pltpu.VMEM((MLA_HEADS, tq, 1), F32),
            pltpu.VMEM((MLA_HEADS, tq, 1), F32),
            pltpu.VMEM((tq, hv), F32),
        ],
        compiler_params=_cparams(("parallel", "parallel", "arbitrary")),
        name="mla_attn",
    )(q, kv, kv, krp)


def _diff_lambda(lam_ref, lam_init):
    lp = lam_ref[...]
    a = jnp.sum(lp[0:1] * lp[1:2], axis=-1, keepdims=True)
    b = jnp.sum(lp[2:3] * lp[3:4], axis=-1, keepdims=True)
    return jnp.exp(a) - jnp.exp(b) + lam_init


def _diff_attn_kernel(q_ref, k_ref, v_ref, lam_ref, o_ref, m_ref, l_ref, acc0_ref, acc1_ref,
                      *, heads, nk, lam_init):
    qi, ki = pl.program_id(1), pl.program_id(2)
    hd2 = LANES

    @pl.when(ki == 0)
    def _():
        m_ref[...] = jnp.full_like(m_ref, NEG_INF)
        l_ref[...] = jnp.zeros_like(l_ref)
        acc0_ref[...] = jnp.zeros_like(acc0_ref)
        acc1_ref[...] = jnp.zeros_like(acc1_ref)

    @pl.when(ki <= qi)
    def _():
        tq, tk = q_ref.shape[0], k_ref.shape[0]
        row = lax.broadcasted_iota(jnp.int32, (tq, tk), 0)
        col = lax.broadcasted_iota(jnp.int32, (tq, tk), 1)
        mask = (row >= col) | (ki < qi)
        lane = lax.broadcasted_iota(jnp.int32, (tq, hd2), 1)
        for h in range(heads):
            lo, hi = h * hd2, (h + 1) * hd2
            qp = q_ref[:, lo:hi]
            kp = k_ref[:, lo:hi].astype(BF16)
            vh = v_ref[:, lo:hi].astype(BF16)
            zero = jnp.zeros_like(qp)
            for c, acc_ref in ((0, acc0_ref), (1, acc1_ref)):
                sel = (lane < hd2 // 2) if c == 0 else (lane >= hd2 // 2)
                qc = jnp.where(sel, qp, zero)
                s = jnp.where(mask, _nt_dot(qc, kp), NEG_INF)
                _flash_update(s, vh, m_ref, l_ref, acc_ref, 2 * h + c, lo, hi)

    @pl.when(ki == nk - 1)
    def _():
        lam = _diff_lambda(lam_ref, lam_init)
        for h in range(heads):
            lo, hi = h * hd2, (h + 1) * hd2
            o_ref[:, lo:hi] = (acc0_ref[:, lo:hi] / l_ref[2 * h]
                               - lam * (acc1_ref[:, lo:hi] / l_ref[2 * h + 1]))


def _diff_attn(q, k, v, lam_arr, layer, lam_init, batch, seq, tq):
    m, w = q.shape
    nq = seq // tq
    kblk = lambda b, qi, ki: (b * nq + jnp.minimum(ki, qi), 0)
    return pl.pallas_call(
        functools.partial(_diff_attn_kernel, heads=DIFF_HEADS, nk=nq, lam_init=lam_init),
        grid=(batch, nq, nq),
        in_specs=[
            pl.BlockSpec((tq, w), lambda b, qi, ki: (b * nq + qi, 0)),
            pl.BlockSpec((tq, w), kblk),
            pl.BlockSpec((tq, w), kblk),
            pl.BlockSpec((None,) + lam_arr.shape[1:], lambda b, qi, ki: (layer, 0, 0)),
        ],
        out_specs=pl.BlockSpec((tq, w), lambda b, qi, ki: (b * nq + qi, 0)),
        out_shape=jax.ShapeDtypeStruct((m, w), F32),
        scratch_shapes=[
            pltpu.VMEM((2 * DIFF_HEADS, tq, 1), F32),
            pltpu.VMEM((2 * DIFF_HEADS, tq, 1), F32),
            pltpu.VMEM((tq, w), F32),
            pltpu.VMEM((tq, w), F32),
        ],
        compiler_params=_cparams(("parallel", "parallel", "arbitrary")),
        name="diff_attn",
    )(q, k, v, lam_arr)


def _dil_attn_kernel(q_ref, kp_ref, kc_ref, vp_ref, vc_ref, o_ref, lse_ref, *, heads, span):
    blk = pl.program_id(2)
    nq = q_ref.shape[0]
    qi = lax.broadcasted_iota(jnp.int32, (nq, 2 * nq), 0)
    kj = lax.broadcasted_iota(jnp.int32, (nq, 2 * nq), 1)
    dist = qi + nq - kj
    mask = (dist >= 0) & (dist <= span) & ((kj >= nq) | (blk > 0))
    for h in range(heads):
        lo, hi = h * LANES, (h + 1) * LANES
        kh = jnp.concatenate([kp_ref[:, lo:hi], kc_ref[:, lo:hi]], axis=0).astype(BF16)
        vh = jnp.concatenate([vp_ref[:, lo:hi], vc_ref[:, lo:hi]], axis=0).astype(BF16)
        s = jnp.where(mask, _nt_dot(q_ref[:, lo:hi], kh), NEG_INF)
        m = jnp.max(s, axis=-1, keepdims=True)
        p = jnp.where(mask, jnp.exp(s - m), 0.0)
        l = jnp.sum(p, axis=-1, keepdims=True)
        o = jnp.dot(p.astype(BF16), vh, preferred_element_type=F32)
        o_ref[:, lo:hi] = o / l
        lse_ref[:, lo:hi] = jnp.broadcast_to(m + jnp.log(l), (nq, LANES))


def _dil_attn(q, kv, batch, seq, w, d):
    hd = q.shape[1]
    ld = seq // d
    nb = ld // DIL_BLOCK
    qv = q.reshape(batch, ld, d * hd)
    kvv = kv.reshape(batch, ld, d * 2 * hd)
    blk = (None, DIL_BLOCK, hd)
    prev = lambda b_: jnp.maximum(b_ - 1, 0)
    o, lse = pl.pallas_call(
        functools.partial(_dil_attn_kernel, heads=DIL_HEADS, span=w // d),
        grid=(batch, d, nb),
        in_specs=[
            pl.BlockSpec(blk, lambda b, r, n: (b, n, r)),
            pl.BlockSpec(blk, lambda b, r, n: (b, prev(n), 2 * r)),
            pl.BlockSpec(blk, lambda b, r, n: (b, n, 2 * r)),
            pl.BlockSpec(blk, lambda b, r, n: (b, prev(n), 2 * r + 1)),
            pl.BlockSpec(blk, lambda b, r, n: (b, n, 2 * r + 1)),
        ],
        out_specs=[pl.BlockSpec(blk, lambda b, r, n: (b, n, r))] * 2,
        out_shape=[jax.ShapeDtypeStruct((batch, ld, d * hd), F32)] * 2,
        compiler_params=_cparams(("parallel", "parallel", "parallel")),
        name="dil_attn",
    )(qv, kvv, kvv, kvv, kvv)
    return o.reshape(batch * seq, hd), lse.reshape(batch * seq, hd)


def _combine_groups(os_, lses):
    m = functools.reduce(jnp.maximum, lses)
    es = [jnp.exp(l - m) for l in lses]
    den = functools.reduce(lambda a, b: a + b, es)
    num = functools.reduce(lambda a, b: a + b, [e * o for e, o in zip(es, os_)])
    return num / den


def _out_kernel(*refs, mode, n_groups, heads, gain_scale):
    x_ref, w_ref = refs[:2]
    o_ref = refs[-1]
    ins = refs[2:-1]
    if mode == "plain":
        a = ins[0][...]
    elif mode == "dil":
        a = _combine_groups([r[...] for r in ins[:n_groups]], [r[...] for r in ins[n_groups:]])
    else:
        o, g = ins[0], ins[1][...]
        parts = []
        for h in range(heads):
            parts.append(_rms(o[:, h * LANES:(h + 1) * LANES], g) * gain_scale)
        a = jnp.concatenate(parts, axis=-1)
    o_ref[...] = x_ref[...] + jnp.dot(a.astype(BF16), w_ref[...], preferred_element_type=F32)


def _out_proj(x, w, layer, ins, tm, mode="plain", gain=None, gain_scale=1.0):
    m, d = x.shape
    kdim = w.shape[1]
    in_specs = [pl.BlockSpec((tm, d), lambda i: (i, 0)),
                pl.BlockSpec((None, kdim, d), lambda i: (layer, 0, 0))]
    args = [x, w]
    for a in ins:
        in_specs.append(pl.BlockSpec((tm, a.shape[1]), lambda i: (i, 0)))
        args.append(a)
    if gain is not None:
        garr, gl = gain
        in_specs.append(pl.BlockSpec((None, 1, garr.shape[-1]), lambda i: (gl, 0, 0)))
        args.append(garr)
    return pl.pallas_call(
        functools.partial(_out_kernel, mode=mode, n_groups=len(DIL_GROUPS), heads=DIFF_HEADS,
                          gain_scale=gain_scale),
        grid=(m // tm,),
        in_specs=in_specs,
        out_specs=pl.BlockSpec((tm, d), lambda i: (i, 0)),
        out_shape=jax.ShapeDtypeStruct((m, d), F32),
        compiler_params=_cparams(("parallel",)),
        name="out_proj",
    )(*args)


def _cross_kernel(x_ref, g_ref, wq_ref, mkv_ref, wo_ref, o_ref, *, heads, scale):
    x = x_ref[...]
    h = _rms(x, g_ref[...]).astype(BF16)
    q = (jnp.dot(h, wq_ref[...], preferred_element_type=F32) * scale).astype(BF16)
    hd = q.shape[1] // heads
    hw = heads * hd
    parts = []
    for hh in range(heads):
        k = mkv_ref[:, hh * hd:(hh + 1) * hd].astype(BF16)
        v = mkv_ref[:, hw + hh * hd:hw + (hh + 1) * hd].astype(BF16)
        s = _nt_dot(q[:, hh * hd:(hh + 1) * hd], k)
        m = jnp.max(s, axis=-1, keepdims=True)
        p = jnp.exp(s - m)
        p = p / jnp.sum(p, axis=-1, keepdims=True)
        parts.append(jnp.dot(p.astype(BF16), v, preferred_element_type=F32).astype(BF16))
    o = jnp.concatenate(parts, axis=-1)
    o_ref[...] = x + jnp.dot(o, wo_ref[...], preferred_element_type=F32)


def _cross_prompt(x, g_arr, gidx, wq, wo, layer, mkv, seq, tm):
    m, d = x.shape
    nrb = seq // tm
    mem_len, kvw = mkv.shape[1], mkv.shape[2]
    hd = d // X_HEADS
    return pl.pallas_call(
        functools.partial(_cross_kernel, heads=X_HEADS, scale=hd ** -0.5),
        grid=(m // tm,),
        in_specs=[
            pl.BlockSpec((tm, d), lambda i: (i, 0)),
            pl.BlockSpec((None, 1, d), lambda i: (gidx, 0, 0)),
            pl.BlockSpec((None, d, d), lambda i: (layer, 0, 0)),
            pl.BlockSpec((None, mem_len, kvw), lambda i: (i // nrb, 0, 0)),
            pl.BlockSpec((None, d, d), lambda i: (layer, 0, 0)),
        ],
        out_specs=pl.BlockSpec((tm, d), lambda i: (i, 0)),
        out_shape=jax.ShapeDtypeStruct((m, d), F32),
        compiler_params=_cparams(("parallel",)),
        name="cross_prompt",
    )(x, g_arr, wq, mkv, wo)


def _block_diag(qrow, rows, head_w):
    w = qrow.shape[1]
    r = lax.broadcasted_iota(jnp.int32, (rows, w), 0)
    lane = lax.broadcasted_iota(jnp.int32, (rows, w), 1)
    sel = (lane >= r * head_w) & (lane < (r + 1) * head_w)
    qb = jnp.where(sel, jnp.broadcast_to(qrow.astype(F32), (rows, w)), 0.0)
    return qb.astype(BF16), sel


def _cross_sample_kernel(q_ref, mkv_ref, o_ref, *, heads):
    w = q_ref.shape[1]
    hd = w // heads
    qb, _ = _block_diag(q_ref[...].astype(BF16), QROWS, hd)
    k = mkv_ref[:, :w].astype(BF16)
    v = mkv_ref[:, w:].astype(BF16)
    s = _nt_dot(qb, k)
    m = jnp.max(s, axis=-1, keepdims=True)
    p = jnp.exp(s - m)
    p = p / jnp.sum(p, axis=-1, keepdims=True)
    of = jnp.dot(p.astype(BF16), v, preferred_element_type=F32)
    r = lax.broadcasted_iota(jnp.int32, of.shape, 0)
    lane = lax.broadcasted_iota(jnp.int32, of.shape, 1)
    sel = (lane >= r * hd) & (lane < (r + 1) * hd)
    o_ref[...] = jnp.sum(jnp.where(sel, of, 0.0), axis=0, keepdims=True).astype(o_ref.dtype)


def _cross_sample(q, mem_kv, layer):
    db, _, w = q.shape
    mem_len = mem_kv.shape[2]
    return pl.pallas_call(
        functools.partial(_cross_sample_kernel, heads=X_HEADS),
        grid=(db,),
        in_specs=[
            pl.BlockSpec((None, 1, w), lambda b: (b, 0, 0)),
            pl.BlockSpec((None, None, mem_len, 2 * w), lambda b: (layer, b, 0, 0)),
        ],
        out_specs=pl.BlockSpec((None, 1, w), lambda b: (b, 0, 0)),
        out_shape=jax.ShapeDtypeStruct((db, 1, w), F32),
        compiler_params=_cparams(("parallel",)),
        name="cross_sample",
    )(q, mem_kv)


def _dil_sample_kernel(*refs, heads, n_groups):
    q_ref, kv_ref = refs[:2]
    bufs = refs[2:2 + n_groups]
    o_ref = refs[2 + n_groups]
    w = heads * LANES
    outs, lses = [], []
    for g in range(n_groups):
        qrow = q_ref[:, g * w:(g + 1) * w].astype(BF16)
        qb, sel = _block_diag(qrow, QROWS, LANES)
        k_new = kv_ref[:, 2 * g * w:(2 * g + 1) * w].astype(BF16)
        v_new = kv_ref[:, (2 * g + 1) * w:(2 * g + 2) * w].astype(BF16)
        k = bufs[g][:, :w].astype(BF16)
        v = bufs[g][:, w:].astype(BF16)
        s = _nt_dot(qb, k)
        s_new = jnp.sum(qb.astype(F32) * k_new.astype(F32), axis=-1, keepdims=True)
        m = jnp.maximum(jnp.max(s, axis=-1, keepdims=True), s_new)
        p = jnp.exp(s - m)
        p_new = jnp.exp(s_new - m)
        l = jnp.sum(p, axis=-1, keepdims=True) + p_new
        of = jnp.dot(p.astype(BF16), v, preferred_element_type=F32) + p_new * v_new.astype(F32)
        outs.append(jnp.sum(jnp.where(sel, of / l, 0.0), axis=0, keepdims=True))
        lses.append(jnp.sum(jnp.where(sel, m + jnp.log(l), 0.0), axis=0, keepdims=True))
    o_ref[...] = _combine_groups(outs, lses).astype(o_ref.dtype)


def _dil_sample(q, kv_new, bufs, layer):
    db = q.shape[0]
    w = DIL_HEADS * LANES
    n_groups = len(DIL_GROUPS)
    in_specs = [
        pl.BlockSpec((None, 1, n_groups * w), lambda b: (b, 0, 0)),
        pl.BlockSpec((None, 1, n_groups * 2 * w), lambda b: (b, 0, 0)),
    ]
    args = [q, kv_new]
    for (wg, d), buf in zip(DIL_GROUPS, bufs):
        wb = buf.shape[2]
        assert wb == wg and wb % d == 0, "window buffer must hold exactly the group's window"
        args.append(buf.reshape(buf.shape[0], db, wb // d, d * 2 * w))
        in_specs.append(pl.BlockSpec((None, None, wb // d, 2 * w), lambda b: (layer, b, 0, 0)))
    return pl.pallas_call(
        functools.partial(_dil_sample_kernel, heads=DIL_HEADS, n_groups=n_groups),
        grid=(db,),
        in_specs=in_specs,
        out_specs=pl.BlockSpec((None, 1, w), lambda b: (b, 0, 0)),
        out_shape=jax.ShapeDtypeStruct((db, 1, w), F32),
        compiler_params=_cparams(("parallel",)),
        name="dil_sample",
    )(*args)


def _mla_absorb_q_kernel(q_ref, wkv_ref, o_ref, *, heads, kv_lora):
    slot_out = kv_lora + LANES
    for h in range(heads):
        qn = q_ref[:, h * MLA_SLOT:h * MLA_SLOT + MLA_NOPE]
        w_uk = wkv_ref[:, h * MLA_NOPE:(h + 1) * MLA_NOPE]
        o_ref[:, h * slot_out:h * slot_out + kv_lora] = _nt_dot(qn, w_uk).astype(o_ref.dtype)
        o_ref[:, h * slot_out + kv_lora:(h + 1) * slot_out] = (
            q_ref[:, h * MLA_SLOT + MLA_NOPE:(h + 1) * MLA_SLOT].astype(o_ref.dtype))


def _mla_absorb_q(q, wkv, layer):
    db = q.shape[0]
    kv_lora, nkv = wkv.shape[1], wkv.shape[2]
    slot_out = kv_lora + LANES
    return pl.pallas_call(
        functools.partial(_mla_absorb_q_kernel, heads=MLA_HEADS, kv_lora=kv_lora),
        grid=(1,),
        in_specs=[pl.BlockSpec(q.shape, lambda i: (0, 0)),
                  pl.BlockSpec((None, kv_lora, nkv), lambda i: (layer, 0, 0))],
        out_specs=pl.BlockSpec((db, MLA_HEADS * slot_out), lambda i: (0, 0)),
        out_shape=jax.ShapeDtypeStruct((db, MLA_HEADS * slot_out), F32),
        compiler_params=_cparams(("arbitrary",)),
        name="mla_absorb_q",
    )(q, wkv)


def _mla_paged_kernel(*refs, pages, kv_lora, row_w):
    pt_ref, q_ref, new_ref = refs[:3]
    page_refs = refs[3:3 + pages]
    o_ref = refs[3 + pages]
    rows_ref, m_ref, l_ref, acc_ref = refs[4 + pages:]
    del pt_ref
    g = pl.program_id(1)
    kpad = rows_ref.shape[1]

    @pl.when(g == 0)
    def _():
        rows_ref[:, row_w:] = jnp.zeros((rows_ref.shape[0], kpad - row_w), BF16)
        new = new_ref[...].astype(BF16).astype(F32)
        qf = q_ref[:, :row_w].astype(BF16).astype(F32)
        m_ref[...] = jnp.sum(qf * new, axis=-1, keepdims=True)
        l_ref[...] = jnp.ones_like(l_ref)
        acc_ref[...] = jnp.broadcast_to(new[:, :kv_lora], acc_ref.shape)

    for k in range(pages):
        rows_ref[k * PAGE_SIZE:(k + 1) * PAGE_SIZE, :row_w] = page_refs[k][...].astype(BF16)
    rows = rows_ref[...]
    s = _nt_dot(q_ref[...].astype(BF16), rows)
    m_prev = m_ref[...]
    m_new = jnp.maximum(m_prev, jnp.max(s, axis=-1, keepdims=True))
    alpha = jnp.exp(m_prev - m_new)
    p = jnp.exp(s - m_new)
    l_ref[...] = alpha * l_ref[...] + jnp.sum(p, axis=-1, keepdims=True)
    acc_ref[...] = alpha * acc_ref[...] + jnp.dot(p.astype(BF16), rows[:, :kv_lora], preferred_element_type=F32)
    m_ref[...] = m_new

    @pl.when(g == pl.num_programs(1) - 1)
    def _():
        o_ref[...] = (acc_ref[...] / l_ref[...]).astype(o_ref.dtype)


def _mla_paged(q_abs, row_new, cache, layer, page_table, pages):
    db, heads, kpad = q_abs.shape
    row_w = cache.shape[-1]
    kv_lora = row_w - MLA_ROPE
    n_pages = page_table.shape[1]
    assert n_pages % pages == 0
    in_specs = [
        pl.BlockSpec((None, heads, kpad), lambda b, g, pt: (b, 0, 0)),
        pl.BlockSpec((None, 1, row_w), lambda b, g, pt: (b, 0, 0)),
    ]
    for k in range(pages):
        in_specs.append(pl.BlockSpec((None, None, PAGE_SIZE, row_w),
                                     lambda b, g, pt, k=k: (layer, pt[b, g * pages + k], 0, 0)))
    return pl.pallas_call(
        functools.partial(_mla_paged_kernel, pages=pages, kv_lora=kv_lora, row_w=row_w),
        grid_spec=pltpu.PrefetchScalarGridSpec(
            num_scalar_prefetch=1,
            grid=(db, n_pages // pages),
            in_specs=in_specs,
            out_specs=pl.BlockSpec((None, heads, kv_lora), lambda b, g, pt: (b, 0, 0)),
            scratch_shapes=[
                pltpu.VMEM((pages * PAGE_SIZE, kpad), BF16),
                pltpu.VMEM((heads, 1), F32),
                pltpu.VMEM((heads, 1), F32),
                pltpu.VMEM((heads, kv_lora), F32),
            ],
        ),
        out_shape=jax.ShapeDtypeStruct((db, heads, kv_lora), F32),
        compiler_params=_cparams(("parallel", "arbitrary")),
        name="mla_paged",
    )(page_table, q_abs, row_new, *([cache] * pages))


def _mla_absorb_o_kernel(o_ref, wkv_ref, out_ref, *, heads, kv_lora):
    v0 = heads * MLA_NOPE
    for h in range(heads):
        w_uv = wkv_ref[:, v0 + h * MLA_V:v0 + (h + 1) * MLA_V]
        out_ref[:, h * MLA_V:(h + 1) * MLA_V] = jnp.dot(
            o_ref[:, h * kv_lora:(h + 1) * kv_lora].astype(BF16), w_uv,
            preferred_element_type=F32).astype(out_ref.dtype)


def _mla_absorb_o(o_lat, wkv, layer):
    db = o_lat.shape[0]
    kv_lora, nkv = wkv.shape[1], wkv.shape[2]
    return pl.pallas_call(
        functools.partial(_mla_absorb_o_kernel, heads=MLA_HEADS, kv_lora=kv_lora),
        grid=(1,),
        in_specs=[pl.BlockSpec(o_lat.shape, lambda i: (0, 0)),
                  pl.BlockSpec((None, kv_lora, nkv), lambda i: (layer, 0, 0))],
        out_specs=pl.BlockSpec((db, MLA_HEADS * MLA_V), lambda i: (0, 0)),
        out_shape=jax.ShapeDtypeStruct((db, MLA_HEADS * MLA_V), BF16),
        compiler_params=_cparams(("arbitrary",)),
        name="mla_absorb_o",
    )(o_lat, wkv)


def _diff_paged_kernel(*refs, pages, heads, lam_init):
    pt_ref, q_ref, kn_ref, vn_ref, lam_ref = refs[:5]
    k_refs = refs[5:5 + pages]
    v_refs = refs[5 + pages:5 + 2 * pages]
    o_ref = refs[5 + 2 * pages]
    qb_ref, m_ref, l_ref, acc_ref = refs[6 + 2 * pages:]
    del pt_ref
    g = pl.program_id(1)
    comps = 2 * heads
    half = LANES // 2

    @pl.when(g == 0)
    def _():
        qb, _ = _block_diag(q_ref[...].astype(BF16), comps, half)
        qb_ref[...] = qb
        kn = kn_ref[...].astype(BF16).astype(F32)
        m_ref[...] = jnp.sum(qb.astype(F32) * kn, axis=-1, keepdims=True)
        l_ref[...] = jnp.ones_like(l_ref)
        acc_ref[...] = jnp.broadcast_to(vn_ref[...].astype(BF16).astype(F32), acc_ref.shape)

    qb = qb_ref[...]
    s = jnp.concatenate([_nt_dot(qb, k_refs[k][...].astype(BF16)) for k in range(pages)], axis=-1)
    m_prev = m_ref[...]
    m_new = jnp.maximum(m_prev, jnp.max(s, axis=-1, keepdims=True))
    alpha = jnp.exp(m_prev - m_new)
    p32 = jnp.exp(s - m_new)
    l_ref[...] = alpha * l_ref[...] + jnp.sum(p32, axis=-1, keepdims=True)
    p = p32.astype(BF16)
    pv =jnp.dot(p[:, :PAGE_SIZE], v_refs[0][...].astype(BF16), preferred_element_type=F32)
    for k in range(1, pages):
        pv += jnp.dot(p[:, k * PAGE_SIZE:(k + 1) * PAGE_SIZE], v_refs[k][...].astype(BF16),
                      preferred_element_type=F32)
    acc_ref[...] = alpha * acc_ref[...] + pv
    m_ref[...] = m_new

    @pl.when(g == pl.num_programs(1) - 1)
    def _():
        lam = _diff_lambda(lam_ref, lam_init)
        of = acc_ref[...] / l_ref[...]
        r = lax.broadcasted_iota(jnp.int32, of.shape, 0)
        head = lax.shift_right_logical(lax.broadcasted_iota(jnp.int32, of.shape, 1), 7)
        pos = jnp.sum(jnp.where(r == 2 * head, of, 0.0), axis=0, keepdims=True)
        neg = jnp.sum(jnp.where(r == 2 * head + 1, of, 0.0), axis=0, keepdims=True)
        o_ref[...] = pos - lam * neg


def _diff_paged(q, k_new, v_new, lam_arr, layer, lam_init, pool_k, pool_v, page_table, pages):
    db, _, w = q.shape
    n_pages = page_table.shape[1]
    assert n_pages % pages == 0
    row = lambda b, g, pt: (b, 0, 0)
    in_specs = [
        pl.BlockSpec((None, 1, w), row),
        pl.BlockSpec((None, 1, w), row),
        pl.BlockSpec((None, 1, w), row),
        pl.BlockSpec((None,) + lam_arr.shape[1:], lambda b, g, pt: (layer, 0, 0)),
    ]
    for _ in range(2):
        for k in range(pages):
            in_specs.append(pl.BlockSpec((None, None, PAGE_SIZE, w),
                                         lambda b, g, pt, k=k: (layer, pt[b, g * pages + k], 0, 0)))
    comps = 2 * DIFF_HEADS
    return pl.pallas_call(
        functools.partial(_diff_paged_kernel, pages=pages, heads=DIFF_HEADS, lam_init=lam_init),
        grid_spec=pltpu.PrefetchScalarGridSpec(
            num_scalar_prefetch=1,
            grid=(db, n_pages // pages),
            in_specs=in_specs,
            out_specs=pl.BlockSpec((None, 1, w), row),
            scratch_shapes=[
                pltpu.VMEM((comps, w), BF16),
                pltpu.VMEM((comps, 1), F32),
                pltpu.VMEM((comps, 1), F32),
                pltpu.VMEM((comps, w), F32),
            ],
        ),
        out_shape=jax.ShapeDtypeStruct((db, 1, w), F32),
        compiler_params=_cparams(("parallel", "arbitrary")),
        name="diff_paged",
    )(page_table, q, k_new, v_new, lam_arr, *([pool_k] * pages), *([pool_v] * pages))


def _norm_kernel(x_ref, g_ref, o_ref):
    o_ref[...] = _rms(x_ref[...], g_ref[...])


def _final_norm(x, g, tm):
    m, d = x.shape
    return pl.pallas_call(
        _norm_kernel,
        grid=(m // tm,),
        in_specs=[pl.BlockSpec((tm, d), lambda i: (i, 0)), pl.BlockSpec((1, d), lambda i: (0, 0))],
        out_specs=pl.BlockSpec((tm, d), lambda i: (i, 0)),
        out_shape=jax.ShapeDtypeStruct((m, d), F32),
        compiler_params=_cparams(("parallel",)),
        name="final_norm",
    )(x, g.reshape(1, d))


def _mla_weights(w_in, w_q_up, w_kv_up):
    n, q_lora = w_q_up.shape[0], w_q_up.shape[1]
    kv_lora = w_kv_up.shape[1]
    win = jnp.concatenate([w_in, w_in[:, :, -MLA_ROPE:]], axis=-1).astype(BF16)
    wq = w_q_up.reshape(n, q_lora, MLA_HEADS, MLA_NOPE + MLA_ROPE)
    pad = jnp.zeros((n, q_lora, MLA_HEADS, MLA_SLOT - MLA_NOPE - MLA_ROPE), w_q_up.dtype)
    wq = jnp.concatenate([wq, pad], axis=-1).reshape(n, q_lora, MLA_HEADS * MLA_SLOT).astype(BF16)
    wkv = w_kv_up.reshape(n, kv_lora, MLA_HEADS, MLA_NOPE + MLA_V)
    wkv = jnp.concatenate([wkv[..., :MLA_NOPE].reshape(n, kv_lora, -1),
                           wkv[..., MLA_NOPE:].reshape(n, kv_lora, -1)], axis=-1).astype(BF16)
    return win, wq, wkv


def kernel(x_prompt, x_sample, cache_mla, cache_dil_g0, cache_dil_g1, cache_dil_g2, cache_diff_k, cache_diff_v, cache_mem_kv, page_table, mem_prompt, norm_g, mem_norm_g, final_norm_g, ffn_w_gu, ffn_w_down, x_wq, x_wkv, x_wo, mla_w_in, mla_q_norm_g, mla_w_q_up, mla_kv_norm_g, mla_w_kv_up, mla_w_out, dil_w_in, dil_w_out, diff_w_in, diff_lambda, diff_subln_g, diff_w_out):
    bp, sp, d = x_prompt.shape
    bs, ts, _ = x_sample.shape
    assert ts == 1, "sample path handles one new token per sequence"
    depth = norm_g.shape[0]
    past_len = page_table.shape[1] * PAGE_SIZE
    n_pg = sp // PAGE_SIZE
    mem_len = mem_prompt.shape[1]
    n_groups = len(DIL_GROUPS)
    dil_bufs = (cache_dil_g0, cache_dil_g1, cache_dil_g2)
    hw = DIL_HEADS * LANES

    tm_p = 512
    tm_s = bs
    tq = 512
    nrb_p = sp // tm_p

    wgu, wdn = ffn_w_gu.astype(BF16), ffn_w_down.astype(BF16)
    wq_x, wkv_x, wo_x = x_wq.astype(BF16), x_wkv.astype(BF16), x_wo.astype(BF16)
    mla_win, mla_wq, mla_wkv = _mla_weights(mla_w_in, mla_w_q_up, mla_w_kv_up)
    mla_wo = mla_w_out.astype(BF16)
    dil_win, dil_wo = dil_w_in.astype(BF16), dil_w_out.astype(BF16)
    diff_win, diff_wo = diff_w_in.astype(BF16), diff_w_out.astype(BF16)

    g_arr = norm_g.reshape(depth * 4, 1, d)
    mem_g = mem_norm_g.reshape(depth, 1, d)
    mla_qg = mla_q_norm_g.reshape(mla_q_norm_g.shape[0], 1, -1)
    mla_kvg = mla_kv_norm_g.reshape(mla_kv_norm_g.shape[0], 1, -1)
    subln_g = diff_subln_g.reshape(diff_subln_g.shape[0], 1, -1)

    pos_p = jnp.arange(sp, dtype=jnp.int32)
    pos_s = jnp.full((bs,), past_len, dtype=jnp.int32)
    rope_p = {dim: _rope_tables(pos_p, dim) for dim in (64, 128)}
    rope_s = {dim: _rope_tables(pos_s, dim) for dim in (64, 128)}

    cache_mem = cache_mem_kv.reshape(depth, bs, mem_len, -1)
    mem2d = mem_prompt.reshape(bp * mem_len, d)
    pool_k = cache_diff_k.reshape(cache_diff_k.shape[0], cache_diff_k.shape[1], PAGE_SIZE, -1)
    pool_v = cache_diff_v.reshape(cache_diff_v.shape[0], cache_diff_v.shape[1], PAGE_SIZE, -1)

    xp = x_prompt.reshape(bp * sp, d)
    xs = x_sample.reshape(bs, d)

    mla_p, mla_s, diff_kp, diff_ks, diff_vp, diff_vs, mem_p = [], [], [], [], [], [], []
    dil_p = [[] for _ in DIL_GROUPS]
    dil_s = [[] for _ in DIL_GROUPS]

    for i in range(depth):
        kind, j = i % N_MIXERS, i // N_MIXERS
        xp = _ffn(xp, g_arr, 4 * i, wgu, wdn, i, 0, tm_p)
        xs = _ffn(xs, g_arr, 4 * i, wgu, wdn, i, 0, tm_s)

        if kind == 0:
            cos, sin = rope_p[MLA_ROPE]
            q, row, krp, kv = _mla_proj(xp, g_arr, 4 * i + 1, mla_win, mla_qg, mla_wq, mla_kvg, mla_wkv,
                                        j, cos, sin, nrb_p, tm_p)
            o_p = _mla_attn(q, kv, krp, bp, sp, tq)
            mla_p.append(row.reshape(bp, n_pg, PAGE_SIZE, -1))
            xp = _out_proj(xp, mla_wo, j, [o_p], tm_p)

            cos, sin = rope_s[MLA_ROPE]
            q, row, _, _ = _mla_proj(xs, g_arr, 4 * i + 1, mla_win, mla_qg, mla_wq, mla_kvg, mla_wkv,
                                     j, cos, sin, 1, tm_s)
            q_abs = _mla_absorb_q(q, mla_wkv, j).reshape(bs, MLA_HEADS, -1)
            o_lat = _mla_paged(q_abs, row.reshape(bs, 1, -1), cache_mla, j, page_table, 16)
            o_s = _mla_absorb_o(o_lat.reshape(bs, -1), mla_wkv, j)
            mla_s.append(row.reshape(bs, ts, -1))
            xs = _out_proj(xs, mla_wo, j, [o_s], tm_s)
        elif kind == 1:
            scale = LANES ** -0.5
            os_p, lses_p, qs_s, kvs_s = [], [], [], []
            for g, (w, dd) in enumerate(DIL_GROUPS):
                cfg = [(0, LANES, scale), (1, LANES, 1.0), (1, 0, 1.0)]
                outs = [(hw, BF16, lambda jj: 0), (2 * hw, F32, lambda jj: jnp.maximum(jj - 1, 0))]
                q, kv = _proj(xp, g_arr, 4 * i + 1, dil_win, j, 3 * g, cfg, outs, tm_p, hw,
                              rope=(*rope_p[LANES], nrb_p))
                o, lse = _dil_attn(q, kv, bp, sp, w, dd)
                os_p.append(o)
                lses_p.append(lse)
                wl = min(w, sp)
                dil_p[g].append(kv.reshape(bp, sp, 2, DIL_HEADS, LANES)[:, sp - wl:])
                q, kv = _proj(xs, g_arr, 4 * i + 1, dil_win, j, 3 * g, cfg, [(hw, F32, outs[0][2]), outs[1]],
                              tm_s, hw, rope=(*rope_s[LANES], 1))
                qs_s.append(q)
                kvs_s.append(kv)
                dil_s[g].append(kv.reshape(bs, ts, 2, DIL_HEADS, LANES))
            xp = _out_proj(xp, dil_wo, j, os_p + lses_p, tm_p, mode="dil")
            q_all = jnp.concatenate(qs_s, axis=-1).reshape(bs, 1, -1)
            kv_all = jnp.concatenate(kvs_s, axis=-1).reshape(bs, 1, -1)
            o_s = _dil_sample(q_all, kv_all, dil_bufs, j)
            xs = _out_proj(xs, dil_wo, j, [o_s.reshape(bs, -1)], tm_s)
        else:
            lam_init = 0.8 - 0.6 * math.exp(-0.3 * i)
            dw = diff_win.shape[-1] // 3
            scale = (LANES // 2) ** -0.5
            cfg = [(0, LANES // 2, scale), (1, LANES // 2, 1.0), (2, 0, 1.0)]
            outs = [(dw, BF16, lambda jj: 0), (dw, F32, lambda jj: 0), (dw, F32, lambda jj: 0)]
            q, k, v = _proj(xp, g_arr, 4 * i + 1, diff_win, j, 0, cfg, outs, tm_p, dw,
                            rope=(*rope_p[LANES // 2], nrb_p))
            o_p = _diff_attn(q, k, v, diff_lambda, j, lam_init, bp, sp, tq)
            diff_kp.append(k.reshape(bp, n_pg, PAGE_SIZE, 2 * DIFF_HEADS, LANES // 2))
            diff_vp.append(v.reshape(bp, n_pg, PAGE_SIZE, DIFF_HEADS, LANES))
            xp = _out_proj(xp, diff_wo, j, [o_p], tm_p, mode="diff", gain=(subln_g, j),
                           gain_scale=1.0 - lam_init)

            q, k, v = _proj(xs, g_arr, 4 * i + 1, diff_win, j, 0, cfg, [(dw, F32, outs[0][2])] + outs[1:],
                            tm_s, dw, rope=(*rope_s[LANES // 2], 1))
            o_s = _diff_paged(q.reshape(bs, 1, dw), k.reshape(bs, 1, dw), v.reshape(bs, 1, dw),
                              diff_lambda, j, lam_init, pool_k, pool_v, page_table, 8)
            diff_ks.append(k.reshape(bs, ts, 2 * DIFF_HEADS, LANES // 2))
            diff_vs.append(v.reshape(bs, ts, DIFF_HEADS, LANES))
            xs = _out_proj(xs, diff_wo, j, [o_s.reshape(bs, dw)], tm_s, mode="diff", gain=(subln_g, j),
                           gain_scale=1.0 - lam_init)

        kvw = wkv_x.shape[-1]
        (mkv,) = _proj(mem2d, mem_g, i, wkv_x, i, 0, [(0, 0, 1.0)] * (kvw // d),
                       [(kvw, F32, lambda jj: jj)], min(tm_p, mem2d.shape[0]), d)
        mem_p.append(mkv.reshape(bp, mem_len, 2, X_HEADS, d // X_HEADS))
        xp = _cross_prompt(xp, g_arr, 4 * i + 2, wq_x, wo_x, i, mkv.reshape(bp, mem_len, kvw), sp, tm_p)
        (q,) = _proj(xs, g_arr, 4 * i + 2, wq_x, i, 0, [(0, 0, (d // X_HEADS) ** -0.5)],
                     [(d, F32, lambda jj: 0)], tm_s, d)
        o_s = _cross_sample(q.reshape(bs, 1, d), cache_mem, i)
        xs = _out_proj(xs, wo_x, i, [o_s.reshape(bs, d)], tm_s)

        xp = _ffn(xp, g_arr, 4 * i + 3, wgu, wdn, i, 1, tm_p)
        xs = _ffn(xs, g_arr, 4 * i + 3, wgu, wdn, i, 1, tm_s)

    yp = _final_norm(xp, final_norm_g, tm_p).reshape(bp, sp, d)
    ys = _final_norm(xs, final_norm_g, tm_s).reshape(bs, ts, d)
    return (yp, ys,
            jnp.stack(mla_p), jnp.stack(mla_s),
            jnp.stack(dil_p[0]), jnp.stack(dil_s[0]),
            jnp.stack(dil_p[1]), jnp.stack(dil_s[1]),
            jnp.stack(dil_p[2]), jnp.stack(dil_s[2]),
            jnp.stack(diff_kp), jnp.stack(diff_ks),
            jnp.stack(diff_vp), jnp.stack(diff_vs),
            jnp.stack(mem_p))
```

```python
import functools
import math

import jax
import jax.numpy as jnp
from jax import lax
from jax.experimental import pallas as pl
from jax.experimental.pallas import tpu as pltpu

F32 = jnp.float32
BF16 = jnp.bfloat16

NORM_EPS = 1e-6
NEG_INF = -1e30
ROPE_THETA = 10000.0
PAGE_SIZE = 128
N_MIXERS = 3

MLA_HEADS = 8
MLA_NOPE = 128
MLA_ROPE = 64
MLA_V = 128
MLA_SLOT = 256

DIL_GROUPS = ((128, 1), (512, 4), (2048, 16))
DIL_HEADS = 8
DIL_BLOCK = 128

DIFF_HEADS = 8
X_HEADS = 4

LANES = 128
QROWS = 16
VMEM_LIMIT = 48 * 1024 * 1024


def _cparams(sem, vmem=VMEM_LIMIT):
    return pltpu.CompilerParams(dimension_semantics=sem, vmem_limit_bytes=vmem)


def _rms(x, g):
    ms = jnp.mean(x * x, axis=-1, keepdims=True)
    return x * lax.rsqrt(ms + NORM_EPS) * g


def _nt_dot(a, b):
    return lax.dot_general(a, b, (((1,), (1,)), ((), ())), preferred_element_type=F32)


def _rope_tables(pos, dim):
    half = dim // 2
    inv = jnp.exp(jnp.arange(half, dtype=F32) * (-2.0 * math.log(ROPE_THETA) / dim))
    ang = pos.astype(F32)[:, None] * inv[None, :]
    cos, sin = jnp.cos(ang), jnp.sin(ang)
    reps = LANES // dim
    cos_t = jnp.tile(jnp.concatenate([cos, cos], axis=-1), (1, reps))
    sin_t = jnp.tile(jnp.concatenate([-sin, sin], axis=-1), (1, reps))
    return cos_t, sin_t


def _apply_rope(chunk, cos, sin, dim):
    half = dim // 2
    if dim == LANES:
        partner = pltpu.roll(chunk, half, 1)
    else:
        lane = lax.broadcasted_iota(jnp.int32, chunk.shape, 1)
        first = (lane & (dim - 1)) < half
        partner = jnp.where(first, pltpu.roll(chunk, LANES - half, 1), pltpu.roll(chunk, half, 1))
    return chunk * cos + partner * sin


def _ffn_kernel(x_ref, g_ref, wg_ref, wu_ref, wd_ref, o_ref, h_ref, acc_ref, *, nf):
    f = pl.program_id(1)

    @pl.when(f == 0)
    def _():
        h_ref[...] = _rms(x_ref[...], g_ref[...]).astype(BF16)
        acc_ref[...] = jnp.zeros_like(acc_ref)

    h = h_ref[...]
    g = jnp.dot(h, wg_ref[...], preferred_element_type=F32)
    u = jnp.dot(h, wu_ref[...], preferred_element_type=F32)
    a = (g * jax.nn.sigmoid(g) * u).astype(BF16)
    acc_ref[...] += jnp.dot(a, wd_ref[...], preferred_element_type=F32)

    @pl.when(f == nf - 1)
    def _():
        o_ref[...] = x_ref[...] + 0.5 * acc_ref[...]


def _ffn_chunk(f_dim):
    cands = [c for c in range(LANES, f_dim + 1, LANES) if f_dim % c == 0 and c <= 1536]
    return max(cands)


def _ffn(x, g_arr, gidx, wgu, wd, layer, which, tm):
    m, d = x.shape
    f_dim = wd.shape[2]
    tf = _ffn_chunk(f_dim)
    nf = f_dim // tf
    return pl.pallas_call(
        functools.partial(_ffn_kernel, nf=nf),
        grid=(m // tm, nf),
        in_specs=[
            pl.BlockSpec((tm, d), lambda i, f: (i, 0)),
            pl.BlockSpec((None, 1, d), lambda i, f: (gidx, 0, 0)),
            pl.BlockSpec((None, None, d, tf), lambda i, f: (layer, which, 0, f)),
            pl.BlockSpec((None, None, d, tf), lambda i, f: (layer, which, 0, nf + f)),
            pl.BlockSpec((None, None, tf, d), lambda i, f: (layer, which, f, 0)),
        ],
        out_specs=pl.BlockSpec((tm, d), lambda i, f: (i, 0)),
        out_shape=jax.ShapeDtypeStruct((m, d), F32),
        scratch_shapes=[pltpu.VMEM((tm, d), BF16), pltpu.VMEM((tm, d), F32)],
        compiler_params=_cparams(("parallel", "arbitrary")),
        name="ffn",
    )(x, g_arr, wgu, wgu, wd)


def _proj_kernel(*refs, cfg, n_out, has_rope, tn):
    x_ref, g_ref, w_ref = refs[:3]
    k = 3
    if has_rope:
        cos_ref, sin_ref = refs[3:5]
        k = 5
    outs = refs[k:k + n_out]
    h_ref = refs[k + n_out]
    j = pl.program_id(1)

    @pl.when(j == 0)
    def _():
        h_ref[...] = _rms(x_ref[...], g_ref[...]).astype(BF16)

    acc = jnp.dot(h_ref[...], w_ref[...], preferred_element_type=F32)
    for jj, (oi, rdim, scale, row0) in enumerate(cfg):

        @pl.when(j == jj)
        def _(oi=oi, rdim=rdim, scale=scale, row0=row0):
            o = outs[oi]
            head_rows = len(o.shape) == 3
            if rdim or head_rows:
                if rdim:
                    cos, sin = cos_ref[...], sin_ref[...]
                for c in range(tn // LANES):
                    ch = acc[:, c * LANES:(c + 1) * LANES]
                    if rdim:
                        ch = _apply_rope(ch, cos, sin, rdim)
                    if scale != 1.0:
                        ch = ch * scale
                    if head_rows:
                        o[row0 + c] = ch.astype(o.dtype)
                    else:
                        o[:, c * LANES:(c + 1) * LANES] = ch.astype(o.dtype)
            else:
                a = acc if scale == 1.0 else acc * scale
                o[...] = a.astype(o.dtype)


def _proj(x, g_arr, gidx, w, wlayer, col0, cfg, out_defs, tm, tn, rope=None):
    m, kdim = x.shape
    nj = len(cfg)
    in_specs = [
        pl.BlockSpec((tm, kdim), lambda i, j: (i, 0)),
        pl.BlockSpec((None, 1, kdim), lambda i, j: (gidx, 0, 0)),
        pl.BlockSpec((None, kdim, tn), lambda i, j: (wlayer, 0, col0 + j)),
    ]
    args = [x, g_arr, w]
    if rope is not None:
        cos, sin, nrb = rope
        in_specs += [pl.BlockSpec((tm, LANES), lambda i, j: (i % nrb, 0))] * 2
        args += [cos, sin]
    out_specs, out_shape = [], []
    for cols, dtype, cb in out_defs:
        if cb is None:
            out_specs.append(pl.BlockSpec((cols, tm, LANES), lambda i, j: (0, i, 0)))
            out_shape.append(jax.ShapeDtypeStruct((cols, m, LANES), dtype))
        else:
            out_specs.append(pl.BlockSpec((tm, tn), lambda i, j, cb=cb: (i, cb(j))))
            out_shape.append(jax.ShapeDtypeStruct((m, cols), dtype))
    return pl.pallas_call(
        functools.partial(_proj_kernel, cfg=tuple(cfg), n_out=len(out_defs), has_rope=rope is not None, tn=tn),
        grid=(m // tm, nj),
        in_specs=in_specs,
        out_specs=out_specs,
        out_shape=out_shape,
        scratch_shapes=[pltpu.VMEM((tm, kdim), BF16)],
        compiler_params=_cparams(("parallel", "arbitrary")),
        name="proj",
    )(*args)


def _mla_proj_kernel(x_ref, g_ref, win_ref, qg_ref, wq_ref, kvg_ref, wkv_ref, cos_ref, sin_ref,
                     q_ref, row_ref, krp_ref, kv_ref, *, q_lora, kv_lora, scale):
    h = _rms(x_ref[...], g_ref[...]).astype(BF16)
    p = jnp.dot(h, win_ref[...], preferred_element_type=F32)
    cq = p[:, :q_lora]
    ckv = p[:, q_lora:q_lora + kv_lora]
    kr2 = p[:, q_lora + kv_lora:]
    cos, sin = cos_ref[...], sin_ref[...]

    qn = _rms(cq, qg_ref[...]).astype(BF16)
    q = jnp.dot(qn, wq_ref[...], preferred_element_type=F32)
    for c in range(q.shape[1] // LANES):
        ch = q[:, c * LANES:(c + 1) * LANES]
        if c % 2 == 1:
            ch = _apply_rope(ch, cos, sin, MLA_ROPE)
        q_ref[:, c * LANES:(c + 1) * LANES] = (ch * scale).astype(q_ref.dtype)

    c_n = _rms(ckv, kvg_ref[...])
    kr_rot = _apply_rope(kr2, cos, sin, MLA_ROPE)
    row_ref[:, :kv_lora] = c_n
    row_ref[:, kv_lora:] = kr_rot[:, :MLA_ROPE]
    lane = lax.broadcasted_iota(jnp.int32, kr_rot.shape, 1)
    krp_ref[...] = jnp.where(lane < MLA_ROPE, kr_rot, 0.0).astype(krp_ref.dtype)
    kv_ref[...] = jnp.dot(c_n.astype(BF16), wkv_ref[...], preferred_element_type=F32).astype(kv_ref.dtype)


def _mla_proj(x, g_arr, gidx, win, qg, wq, kvg, wkv, layer, cos, sin, nrb, tm):
    m, d = x.shape
    q_lora, kv_lora = qg.shape[-1], kvg.shape[-1]
    n_in = win.shape[-1]
    nq = wq.shape[-1]
    nkv = wkv.shape[-1]
    scale = (MLA_NOPE + MLA_ROPE) ** -0.5
    return pl.pallas_call(
        functools.partial(_mla_proj_kernel, q_lora=q_lora, kv_lora=kv_lora, scale=scale),
        grid=(m // tm,),
        in_specs=[
            pl.BlockSpec((tm, d), lambda i: (i, 0)),
            pl.BlockSpec((None, 1, d), lambda i: (gidx, 0, 0)),
            pl.BlockSpec((None, d, n_in), lambda i: (layer, 0, 0)),
            pl.BlockSpec((None, 1, q_lora), lambda i: (layer, 0, 0)),
            pl.BlockSpec((None, q_lora, nq), lambda i: (layer, 0, 0)),
            pl.BlockSpec((None, 1, kv_lora), lambda i: (layer, 0, 0)),
            pl.BlockSpec((None, kv_lora, nkv), lambda i: (layer, 0, 0)),
            pl.BlockSpec((tm, LANES), lambda i: (i % nrb, 0)),
            pl.BlockSpec((tm, LANES), lambda i: (i % nrb, 0)),
        ],
        out_specs=[
            pl.BlockSpec((tm, nq), lambda i: (i, 0)),
            pl.BlockSpec((tm, kv_lora + MLA_ROPE), lambda i: (i, 0)),
            pl.BlockSpec((tm, LANES), lambda i: (i, 0)),
            pl.BlockSpec((tm, nkv), lambda i: (i, 0)),
        ],
        out_shape=[
            jax.ShapeDtypeStruct((m, nq), BF16),
            jax.ShapeDtypeStruct((m, kv_lora + MLA_ROPE), F32),
            jax.ShapeDtypeStruct((m, LANES), BF16),
            jax.ShapeDtypeStruct((m, nkv), BF16),
        ],
        compiler_params=_cparams(("parallel",)),
        name="mla_proj",
    )(x, g_arr, win, qg, wq, kvg, wkv, cos, sin)


def _flash_update(s, v, m_ref, l_ref, acc_ref, idx, lo, hi):
    m_prev = m_ref[idx]
    m_new = jnp.maximum(m_prev, jnp.max(s, axis=-1, keepdims=True))
    alpha = jnp.exp(m_prev - m_new)
    p = jnp.exp(s - jnp.tile(m_new, (1, s.shape[1] // LANES)))
    l_ref[idx] = alpha * l_ref[idx] + jnp.sum(p, axis=-1, keepdims=True)
    acc_ref[:, lo:hi] = alpha * acc_ref[:, lo:hi] + jnp.dot(p.astype(BF16), v, preferred_element_type=F32)
    m_ref[idx] = m_new


def _causal_mask(tq, tk):
    row = lax.broadcasted_iota(jnp.int32, (tq, tk), 0)
    col = lax.broadcasted_iota(jnp.int32, (tq, tk), 1)
    return row >= col


def _mla_attn_kernel(q_ref, kn_ref, v_ref, krp_ref, o_ref, m_ref, l_ref, acc_ref, *, heads, nk):
    qi, ki = pl.program_id(1), pl.program_id(2)

    @pl.when(ki == 0)
    def _():
        m_ref[...] = jnp.full_like(m_ref, NEG_INF)
        l_ref[...] = jnp.zeros_like(l_ref)
        acc_ref[...] = jnp.zeros_like(acc_ref)

    def step(diagonal):
        mask = _causal_mask(q_ref.shape[0], kn_ref.shape[0]) if diagonal else None
        krp = krp_ref[...]
        for h in range(heads):
            qh = q_ref[:, h * MLA_SLOT:(h + 1) * MLA_SLOT]
            kh = jnp.concatenate([kn_ref[:, h * MLA_NOPE:(h + 1) * MLA_NOPE], krp], axis=-1)
            s = _nt_dot(qh, kh)
            if diagonal:
                s = jnp.where(mask, s, NEG_INF)
            _flash_update(s, v_ref[:, h * MLA_V:(h + 1) * MLA_V], m_ref, l_ref, acc_ref,
                          h, h * MLA_V, (h + 1) * MLA_V)

    pl.when(ki < qi)(functools.partial(step, False))
    pl.when(ki == qi)(functools.partial(step, True))

    @pl.when(ki == nk - 1)
    def _():
        for h in range(heads):
            lo, hi = h * MLA_V, (h + 1) * MLA_V
            o_ref[:, lo:hi] = (acc_ref[:, lo:hi] / l_ref[h]).astype(o_ref.dtype)


def _mla_attn(q, kv, krp, batch, seq, tq):
    m = q.shape[0]
    nq = seq // tq
    hv = MLA_HEADS * MLA_V
    kblk = lambda b, qi, ki: b * nq + jnp.minimum(ki, qi)
    return pl.pallas_call(
        functools.partial(_mla_attn_kernel, heads=MLA_HEADS, nk=nq),
        grid=(batch, nq, nq),
        in_specs=[
            pl.BlockSpec((tq, MLA_HEADS * MLA_SLOT), lambda b, qi, ki: (b * nq + qi, 0)),
            pl.BlockSpec((tq, MLA_HEADS * MLA_NOPE), lambda b, qi, ki: (kblk(b, qi, ki), 0)),
            pl.BlockSpec((tq, hv), lambda b, qi, ki: (kblk(b, qi, ki), 1)),
            pl.BlockSpec((tq, LANES), lambda b, qi, ki: (kblk(b, qi, ki), 0)),
        ],
        out_specs=pl.BlockSpec((tq, hv), lambda b, qi, ki: (b * nq + qi, 0)),
        out_shape=jax.ShapeDtypeStruct((m, hv), BF16),
        scratch_shapes=[
            pltpu.VMEM((MLA_HEADS, tq, LANES), F32),
            pltpu.VMEM((MLA_HEADS, tq, LANES), F32),
            pltpu.VMEM((tq, hv), F32),
        ],
        compiler_params=_cparams(("parallel", "parallel", "arbitrary")),
        name="mla_attn",
    )(q, kv, kv, krp)


def _diff_lambda(lam_ref, lam_init):
    lp = lam_ref[...]
    a = jnp.sum(lp[0:1] * lp[1:2], axis=-1, keepdims=True)
    b = jnp.sum(lp[2:3] * lp[3:4], axis=-1, keepdims=True)
    return jnp.exp(a) - jnp.exp(b) + lam_init


def _diff_attn_kernel(q_ref, k_ref, v_ref, lam_ref, o_ref, m_ref, l_ref, acc0_ref, acc1_ref,
                      *, heads, nk, lam_init):
    qi, ki = pl.program_id(1), pl.program_id(2)
    hd2 = LANES

    @pl.when(ki == 0)
    def _():
        m_ref[...] = jnp.full_like(m_ref, NEG_INF)
        l_ref[...] = jnp.zeros_like(l_ref)
        acc0_ref[...] = jnp.zeros_like(acc0_ref)
        acc1_ref[...] = jnp.zeros_like(acc1_ref)

    def step(diagonal):
        tq = q_ref.shape[0]
        mask = _causal_mask(tq, k_ref.shape[0]) if diagonal else None
        lane = lax.broadcasted_iota(jnp.int32, (tq, hd2), 1)
        for h in range(heads):
            lo, hi = h * hd2, (h + 1) * hd2
            qp = q_ref[:, lo:hi].astype(F32)
            kp = k_ref[:, lo:hi].astype(BF16)
            vh = v_ref[:, lo:hi].astype(BF16)
            for c, acc_ref in ((0, acc0_ref), (1, acc1_ref)):
                sel = (lane < hd2 // 2) if c == 0 else (lane >= hd2 // 2)
                qc = jnp.where(sel, qp, 0.0).astype(BF16)
                s = _nt_dot(qc, kp)
                if diagonal:
                    s = jnp.where(mask, s, NEG_INF)
                _flash_update(s, vh, m_ref, l_ref, acc_ref, 2 * h + c, lo, hi)

    pl.when(ki < qi)(functools.partial(step, False))
    pl.when(ki == qi)(functools.partial(step, True))

    @pl.when(ki == nk - 1)
    def _():
        lam = _diff_lambda(lam_ref, lam_init)
        for h in range(heads):
            lo, hi = h * hd2, (h + 1) * hd2
            o_ref[:, lo:hi] = (acc0_ref[:, lo:hi] / l_ref[2 * h]
                               - lam * (acc1_ref[:, lo:hi] / l_ref[2 * h + 1]))


def _diff_attn(q, k, v, lam_arr, layer, lam_init, batch, seq, tq):
    m, w = q.shape
    nq = seq // tq
    kblk = lambda b, qi, ki: (b * nq + jnp.minimum(ki, qi), 0)
    return pl.pallas_call(
        functools.partial(_diff_attn_kernel, heads=DIFF_HEADS, nk=nq, lam_init=lam_init),
        grid=(batch, nq, nq),
        in_specs=[
            pl.BlockSpec((tq, w), lambda b, qi, ki: (b * nq + qi, 0)),
            pl.BlockSpec((tq, w), kblk),
            pl.BlockSpec((tq, w), kblk),
            pl.BlockSpec((None,) + lam_arr.shape[1:], lambda b, qi, ki: (layer, 0, 0)),
        ],
        out_specs=pl.BlockSpec((tq, w), lambda b, qi, ki: (b * nq + qi, 0)),
        out_shape=jax.ShapeDtypeStruct((m, w), F32),
        scratch_shapes=[
            pltpu.VMEM((2 * DIFF_HEADS, tq, LANES), F32),
            pltpu.VMEM((2 * DIFF_HEADS, tq, LANES), F32),
            pltpu.VMEM((tq, w), F32),
            pltpu.VMEM((tq, w), F32),
        ],
        compiler_params=_cparams(("parallel", "parallel", "arbitrary")),
        name="diff_attn",
    )(q, k, v, lam_arr)


def _dil_attn_kernel(*refs, heads, d, nbs, span, has_prev):
    if has_prev:
        q_ref, kc_ref, vc_ref, kp_ref, vp_ref, o_ref, lse_ref = refs
    else:
        q_ref, kc_ref, vc_ref, o_ref, lse_ref = refs
    first = pl.program_id(1) == 0
    nq = DIL_BLOCK
    nkeys = 2 * nq if has_prev else nq
    qi = lax.broadcasted_iota(jnp.int32, (nq, nkeys), 0)
    kj = lax.broadcasted_iota(jnp.int32, (nq, nkeys), 1)
    dist = qi + (nkeys - nq) - kj
    band = (dist >= 0) & (dist <= span)
    rows = lambda r, i: pl.ds(r + d * nq * i, nq, stride=d) if d > 1 else pl.ds(nq * i, nq)
    for r in range(d):
        for i in range(nbs):
            cur = rows(r, i)
            mask = band
            if has_prev and i == 0:
                mask = band & ((kj >= nq) | jnp.logical_not(first))
            for h in range(heads):
                kh, vh = kc_ref[h, cur, :], vc_ref[h, cur, :]
                if has_prev:
                    if i == 0:
                        kprev, vprev = kp_ref[h, rows(r, 0), :], vp_ref[h, rows(r, 0), :]
                    else:
                        kprev, vprev = kc_ref[h, rows(r, i - 1), :], vc_ref[h, rows(r, i - 1), :]
                    kh = jnp.concatenate([kprev, kh], axis=0)
                    vh = jnp.concatenate([vprev, vh], axis=0)
                s = jnp.where(mask, _nt_dot(q_ref[h, cur, :].astype(BF16), kh.astype(BF16)), NEG_INF)
                m = jnp.max(s, axis=-1, keepdims=True)
                p = jnp.where(mask, jnp.exp(s - m), 0.0)
                l = jnp.sum(p, axis=-1, keepdims=True)
                o = jnp.dot(p.astype(BF16), vh.astype(BF16), preferred_element_type=F32)
                o_ref[h, cur, :] = o / l
                lse_ref[h, cur, :] = jnp.broadcast_to(m + jnp.log(l), (nq, LANES))


def _dil_attn(q, kv, batch, seq, w, d):
    heads, m, hd = q.shape
    stride_rows = DIL_BLOCK * d
    n_rb = seq // stride_rows
    has_prev = n_rb > 1
    nbs = min(n_rb, max(1, 512 // stride_rows))
    tb = stride_rows * nbs
    steps = seq // tb
    hc = max(1, min(heads, 32 // (d * nbs)))
    n_hc = heads // hc
    cur = lambda b, n, c: (c, b * steps + n, 0)
    cur_v = lambda b, n, c: (n_hc + c, b * steps + n, 0)
    blk = (hc, tb, hd)
    in_specs = [pl.BlockSpec(blk, cur), pl.BlockSpec(blk, cur), pl.BlockSpec(blk, cur_v)]
    args = [q, kv, kv]
    if has_prev:
        prev = lambda b, n: b * n_rb + jnp.maximum(n * nbs - 1, 0)
        in_specs += [pl.BlockSpec((hc, stride_rows, hd), lambda b, n, c: (c, prev(b, n), 0)),
                     pl.BlockSpec((hc, stride_rows, hd), lambda b, n, c: (n_hc + c, prev(b, n), 0))]
        args += [kv, kv]
    return pl.pallas_call(
        functools.partial(_dil_attn_kernel, heads=hc, d=d, nbs=nbs, span=w // d, has_prev=has_prev),
        grid=(batch, steps, n_hc),
        in_specs=in_specs,
        out_specs=[pl.BlockSpec(blk, cur)] * 2,
        out_shape=[jax.ShapeDtypeStruct(q.shape, F32)] * 2,
        compiler_params=_cparams(("parallel", "parallel", "parallel")),
        name="dil_attn",
    )(*args)


def _combine_groups(os_, lses):
    m = functools.reduce(jnp.maximum, lses)
    es = [jnp.exp(l - m) for l in lses]
    den = functools.reduce(lambda a, b: a + b, es)
    num = functools.reduce(lambda a, b: a + b, [e * o for e, o in zip(es, os_)])
    return num / den


def _out_kernel(*refs, mode, n_groups, heads, gain_scale):
    x_ref, w_ref = refs[:2]
    o_ref = refs[-1]
    ins = refs[2:-1]
    if mode == "plain":
        a = ins[0][...]
    elif mode == "dil":
        parts = []
        for h in range(ins[0].shape[0]):
            parts.append(_combine_groups([r[h] for r in ins[:n_groups]], [r[h] for r in ins[n_groups:]]))
        a = jnp.concatenate(parts, axis=-1)
    else:
        o, g = ins[0], ins[1][...]
        parts = []
        for h in range(heads):
            parts.append(_rms(o[:, h * LANES:(h + 1) * LANES], g) * gain_scale)
        a = jnp.concatenate(parts, axis=-1)
    o_ref[...] = x_ref[...] + jnp.dot(a.astype(BF16), w_ref[...], preferred_element_type=F32)


def _out_proj(x, w, layer, ins, tm, mode="plain", gain=None, gain_scale=1.0):
    m, d = x.shape
    kdim = w.shape[1]
    in_specs = [pl.BlockSpec((tm, d), lambda i: (i, 0)),
                pl.BlockSpec((None, kdim, d), lambda i: (layer, 0, 0))]
    args = [x, w]
    for a in ins:
        if a.ndim == 3:
            in_specs.append(pl.BlockSpec((a.shape[0], tm, a.shape[2]), lambda i: (0, i, 0)))
        else:
            in_specs.append(pl.BlockSpec((tm, a.shape[1]), lambda i: (i, 0)))
        args.append(a)
    if gain is not None:
        garr, gl = gain
        in_specs.append(pl.BlockSpec((None, 1, garr.shape[-1]), lambda i: (gl, 0, 0)))
        args.append(garr)
    return pl.pallas_call(
        functools.partial(_out_kernel, mode=mode, n_groups=len(DIL_GROUPS), heads=DIFF_HEADS,
                          gain_scale=gain_scale),
        grid=(m // tm,),
        in_specs=in_specs,
        out_specs=pl.BlockSpec((tm, d), lambda i: (i, 0)),
        out_shape=jax.ShapeDtypeStruct((m, d), F32),
        compiler_params=_cparams(("parallel",)),
        name="out_proj",
    )(*args)


def _cross_kernel(x_ref, g_ref, wq_ref, mkv_ref, wo_ref, o_ref, *, heads, scale):
    x = x_ref[...]
    h = _rms(x, g_ref[...]).astype(BF16)
    q = (jnp.dot(h, wq_ref[...], preferred_element_type=F32) * scale).astype(BF16)
    hd = q.shape[1] // heads
    hw = heads * hd
    parts = []
    for hh in range(heads):
        k = mkv_ref[:, hh * hd:(hh + 1) * hd].astype(BF16)
        v = mkv_ref[:, hw + hh * hd:hw + (hh + 1) * hd].astype(BF16)
        s = _nt_dot(q[:, hh * hd:(hh + 1) * hd], k)
        m = jnp.max(s, axis=-1, keepdims=True)
        p = jnp.exp(s - m)
        p = p / jnp.sum(p, axis=-1, keepdims=True)
        parts.append(jnp.dot(p.astype(BF16), v, preferred_element_type=F32).astype(BF16))
    o = jnp.concatenate(parts, axis=-1)
    o_ref[...] = x + jnp.dot(o, wo_ref[...], preferred_element_type=F32)


def _cross_prompt(x, g_arr, gidx, wq, wo, layer, mkv, seq, tm):
    m, d = x.shape
    nrb = seq // tm
    mem_len, kvw = mkv.shape[1], mkv.shape[2]
    hd = d // X_HEADS
    return pl.pallas_call(
        functools.partial(_cross_kernel, heads=X_HEADS, scale=hd ** -0.5),
        grid=(m // tm,),
        in_specs=[
            pl.BlockSpec((tm, d), lambda i: (i, 0)),
            pl.BlockSpec((None, 1, d), lambda i: (gidx, 0, 0)),
            pl.BlockSpec((None, d, d), lambda i: (layer, 0, 0)),
            pl.BlockSpec((None, mem_len, kvw), lambda i: (i // nrb, 0, 0)),
            pl.BlockSpec((None, d, d), lambda i: (layer, 0, 0)),
        ],
        out_specs=pl.BlockSpec((tm, d), lambda i: (i, 0)),
        out_shape=jax.ShapeDtypeStruct((m, d), F32),
        compiler_params=_cparams(("parallel",)),
        name="cross_prompt",
    )(x, g_arr, wq, mkv, wo)


def _block_diag(qrow, rows, head_w):
    w = qrow.shape[1]
    r = lax.broadcasted_iota(jnp.int32, (rows, w), 0)
    lane = lax.broadcasted_iota(jnp.int32, (rows, w), 1)
    sel = (lane >= r * head_w) & (lane < (r + 1) * head_w)
    qb = jnp.where(sel, jnp.broadcast_to(qrow.astype(F32), (rows, w)), 0.0)
    return qb.astype(BF16), sel


def _cross_sample_kernel(q_ref, mkv_ref, o_ref, *, heads):
    w = q_ref.shape[1]
    hd = w // heads
    qb, _ = _block_diag(q_ref[...].astype(BF16), QROWS, hd)
    k = mkv_ref[:, :w].astype(BF16)
    v = mkv_ref[:, w:].astype(BF16)
    s = _nt_dot(qb, k)
    m = jnp.max(s, axis=-1, keepdims=True)
    p = jnp.exp(s - m)
    p = p / jnp.sum(p, axis=-1, keepdims=True)
    of = jnp.dot(p.astype(BF16), v, preferred_element_type=F32)
    r = lax.broadcasted_iota(jnp.int32, of.shape, 0)
    lane = lax.broadcasted_iota(jnp.int32, of.shape, 1)
    sel = (lane >= r * hd) & (lane < (r + 1) * hd)
    o_ref[...] = jnp.sum(jnp.where(sel, of, 0.0), axis=0, keepdims=True).astype(o_ref.dtype)


def _cross_sample(q, mem_kv, layer):
    db, _, w = q.shape
    mem_len = mem_kv.shape[2]
    return pl.pallas_call(
        functools.partial(_cross_sample_kernel, heads=X_HEADS),
        grid=(db,),
        in_specs=[
            pl.BlockSpec((None, 1, w), lambda b: (b, 0, 0)),
            pl.BlockSpec((None, None, mem_len, 2 * w), lambda b: (layer, b, 0, 0)),
        ],
        out_specs=pl.BlockSpec((None, 1, w), lambda b: (b, 0, 0)),
        out_shape=jax.ShapeDtypeStruct((db, 1, w), F32),
        compiler_params=_cparams(("parallel",)),
        name="cross_sample",
    )(q, mem_kv)


def _bf16_round(x):
    return x.astype(BF16).astype(F32)


def _dil_sample_kernel(*refs, n_groups):
    q_ref, kv_ref = refs[:2]
    bufs = refs[2:2 + n_groups]
    o_ref = refs[2 + n_groups]
    outs, lses = [], []
    for g in range(n_groups):
        q = _bf16_round(q_ref[g])
        k_new = _bf16_round(kv_ref[2 * g])
        v_new = _bf16_round(kv_ref[2 * g + 1])
        k = _bf16_round(bufs[g][:, 0])
        v = _bf16_round(bufs[g][:, 1])
        s = jnp.sum(k * q[None], axis=-1, keepdims=True)
        s_new = jnp.sum(q * k_new, axis=-1, keepdims=True)
        m = jnp.maximum(jnp.max(s, axis=0), s_new)
        p = jnp.exp(s - m[None])
        p_new = jnp.exp(s_new - m)
        l = jnp.sum(p, axis=0) + p_new
        o = jnp.sum(_bf16_round(p) * v, axis=0) + _bf16_round(p_new) * v_new
        outs.append(o / l)
        lses.append(m + jnp.log(l))
    o_ref[...] = _combine_groups(outs, lses)


def _dil_sample(q, kv_new, bufs, layer):
    db, n_groups, heads, hd = q.shape
    in_specs = [
        pl.BlockSpec((None, n_groups, heads, hd), lambda b: (b, 0, 0, 0)),
        pl.BlockSpec((None, 2 * n_groups, heads, hd), lambda b: (b, 0, 0, 0)),
    ]
    args = [q, kv_new]
    for (wg, d), buf in zip(DIL_GROUPS, bufs):
        wb = buf.shape[2]
        assert wb == wg and wb % d == 0, "window buffer must hold exactly the group's window"
        args.append(buf.reshape(buf.shape[0], db, wb // d, d, 2, heads, hd))
        in_specs.append(pl.BlockSpec((None, None, wb // d, None, 2, heads, hd),
                                     lambda b: (layer, b, 0, 0, 0, 0, 0)))
    return pl.pallas_call(
        functools.partial(_dil_sample_kernel, n_groups=n_groups),
        grid=(db,),
        in_specs=in_specs,
        out_specs=pl.BlockSpec((None, heads, hd), lambda b: (b, 0, 0)),
        out_shape=jax.ShapeDtypeStruct((db, heads, hd), F32),
        compiler_params=_cparams(("parallel",)),
        name="dil_sample",
    )(*args)


def _mla_absorb_q_kernel(q_ref, wkv_ref, o_ref, *, heads, kv_lora):
    slot_out = kv_lora + LANES
    for h in range(heads):
        qn = q_ref[:, h * MLA_SLOT:h * MLA_SLOT + MLA_NOPE]
        w_uk = wkv_ref[:, h * MLA_NOPE:(h + 1) * MLA_NOPE]
        o_ref[:, h * slot_out:h * slot_out + kv_lora] = _nt_dot(qn, w_uk).astype(o_ref.dtype)
        o_ref[:, h * slot_out + kv_lora:(h + 1) * slot_out] = (
            q_ref[:, h * MLA_SLOT + MLA_NOPE:(h + 1) * MLA_SLOT].astype(o_ref.dtype))


def _mla_absorb_q(q, wkv, layer):
    db = q.shape[0]
    kv_lora, nkv = wkv.shape[1], wkv.shape[2]
    slot_out = kv_lora + LANES
    return pl.pallas_call(
        functools.partial(_mla_absorb_q_kernel, heads=MLA_HEADS, kv_lora=kv_lora),
        grid=(1,),
        in_specs=[pl.BlockSpec(q.shape, lambda i: (0, 0)),
                  pl.BlockSpec((None, kv_lora, nkv), lambda i: (layer, 0, 0))],
        out_specs=pl.BlockSpec((db, MLA_HEADS * slot_out), lambda i: (0, 0)),
        out_shape=jax.ShapeDtypeStruct((db, MLA_HEADS * slot_out), F32),
        compiler_params=_cparams(("arbitrary",)),
        name="mla_absorb_q",
    )(q, wkv)


def _mla_paged_kernel(*refs, pages, kv_lora, row_w):
    pt_ref, q_ref, new_ref = refs[:3]
    page_refs = refs[3:3 + pages]
    o_ref = refs[3 + pages]
    rows_ref, m_ref, l_ref, acc_ref = refs[4 + pages:]
    del pt_ref
    g = pl.program_id(1)
    kpad = rows_ref.shape[0]

    @pl.when(g == 0)
    def _():
        rows_ref[row_w:, :] = jnp.zeros((kpad - row_w, rows_ref.shape[1]), BF16)
        new = _bf16_round(new_ref[...])
        qf = _bf16_round(q_ref[:, :row_w])
        m_ref[...] = jnp.sum(qf * new, axis=-1, keepdims=True)
        l_ref[...] = jnp.ones_like(l_ref)
        acc_ref[...] = jnp.broadcast_to(new[:, :kv_lora], acc_ref.shape)

    for k in range(pages):
        rows_ref[:row_w, k * PAGE_SIZE:(k + 1) * PAGE_SIZE] = page_refs[k][...].astype(BF16)
    s = jnp.dot(q_ref[...].astype(BF16), rows_ref[...], preferred_element_type=F32)
    m_prev = m_ref[...]
    m_new = jnp.maximum(m_prev, jnp.max(s, axis=-1, keepdims=True))
    alpha = jnp.exp(m_prev - m_new)
    p = jnp.exp(s - m_new)
    l_ref[...] = alpha * l_ref[...] + jnp.sum(p, axis=-1, keepdims=True)
    acc_ref[...] = alpha * acc_ref[...] + _nt_dot(p.astype(BF16), rows_ref[:kv_lora, :])
    m_ref[...] = m_new

    @pl.when(g == pl.num_programs(1) - 1)
    def _():
        o_ref[...] = (acc_ref[...] / l_ref[...]).astype(o_ref.dtype)


def _mla_paged(q_abs, row_new, cache, layer, page_table, pages):
    db, heads, kpad = q_abs.shape
    row_w = cache.shape[2]
    kv_lora = row_w - MLA_ROPE
    n_pages = page_table.shape[1]
    assert n_pages % pages == 0
    in_specs = [
        pl.BlockSpec((None, heads, kpad), lambda b, g, pt: (b, 0, 0)),
        pl.BlockSpec((None, 1, row_w), lambda b, g, pt: (b, 0, 0)),
    ]
    for k in range(pages):
        in_specs.append(pl.BlockSpec((None, None, row_w, PAGE_SIZE),
                                     lambda b, g, pt, k=k: (layer, pt[b, g * pages + k], 0, 0)))
    return pl.pallas_call(
        functools.partial(_mla_paged_kernel, pages=pages, kv_lora=kv_lora, row_w=row_w),
        grid_spec=pltpu.PrefetchScalarGridSpec(
            num_scalar_prefetch=1,
            grid=(db, n_pages // pages),
            in_specs=in_specs,
            out_specs=pl.BlockSpec((None, heads, kv_lora), lambda b, g, pt: (b, 0, 0)),
            scratch_shapes=[
                pltpu.VMEM((kpad, pages * PAGE_SIZE), BF16),
                pltpu.VMEM((heads, 1), F32),
                pltpu.VMEM((heads, 1), F32),
                pltpu.VMEM((heads, kv_lora), F32),
            ],
        ),
        out_shape=jax.ShapeDtypeStruct((db, heads, kv_lora), F32),
        compiler_params=_cparams(("parallel", "arbitrary")),
        name="mla_paged",
    )(page_table, q_abs, row_new, *([cache] * pages))


def _mla_absorb_o_kernel(o_ref, wkv_ref, out_ref, *, heads, kv_lora):
    v0 = heads * MLA_NOPE
    for h in range(heads):
        w_uv = wkv_ref[:, v0 + h * MLA_V:v0 + (h + 1) * MLA_V]
        out_ref[:, h * MLA_V:(h + 1) * MLA_V] = jnp.dot(
            o_ref[:, h * kv_lora:(h + 1) * kv_lora].astype(BF16), w_uv,
            preferred_element_type=F32).astype(out_ref.dtype)


def _mla_absorb_o(o_lat, wkv, layer):
    db = o_lat.shape[0]
    kv_lora, nkv = wkv.shape[1], wkv.shape[2]
    return pl.pallas_call(
        functools.partial(_mla_absorb_o_kernel, heads=MLA_HEADS, kv_lora=kv_lora),
        grid=(1,),
        in_specs=[pl.BlockSpec(o_lat.shape, lambda i: (0, 0)),
                  pl.BlockSpec((None, kv_lora, nkv), lambda i: (layer, 0, 0))],
        out_specs=pl.BlockSpec((db, MLA_HEADS * MLA_V), lambda i: (0, 0)),
        out_shape=jax.ShapeDtypeStruct((db, MLA_HEADS * MLA_V), BF16),
        compiler_params=_cparams(("arbitrary",)),
        name="mla_absorb_o",
    )(o_lat, wkv)


def _diff_paged_kernel(*refs, pages, heads, lam_init):
    pt_ref, q_ref, kn_ref, vn_ref, lam_ref = refs[:5]
    k_refs = refs[5:5 + pages]
    v_refs = refs[5 + pages:5 + 2 * pages]
    o_ref = refs[5 + 2 * pages]
    qb_ref, m_ref, l_ref, acc_ref = refs[6 + 2 * pages:]
    del pt_ref
    g = pl.program_id(1)
    comps = 2 * heads
    half = LANES // 2

    @pl.when(g == 0)
    def _():
        qb, _ = _block_diag(q_ref[...].astype(BF16), comps, half)
        qb_ref[...] = qb
        kn = kn_ref[...].astype(BF16).astype(F32)
        m_ref[...] = jnp.sum(qb.astype(F32) * kn, axis=-1, keepdims=True)
        l_ref[...] = jnp.ones_like(l_ref)
        acc_ref[...] = jnp.broadcast_to(vn_ref[...].astype(BF16).astype(F32), acc_ref.shape)

    qb = qb_ref[...]
    s = jnp.concatenate([jnp.dot(qb, k_refs[k][...].astype(BF16), preferred_element_type=F32)
                         for k in range(pages)], axis=-1)
    m_prev = m_ref[...]
    m_new = jnp.maximum(m_prev, jnp.max(s, axis=-1, keepdims=True))
    alpha = jnp.exp(m_prev - m_new)
    p32 = jnp.exp(s - m_new)
    l_ref[...] = alpha * l_ref[...] + jnp.sum(p32, axis=-1, keepdims=True)
    p = p32.astype(BF16)
    for h in range(heads):
        vh = jnp.concatenate([v_refs[k][:, h, :] for k in range(pages)], axis=0).astype(BF16)
        lo, hi = h * LANES, (h + 1) * LANES
        acc_ref[:, lo:hi] = alpha * acc_ref[:, lo:hi] + jnp.dot(p, vh, preferred_element_type=F32)
    m_ref[...] = m_new

    @pl.when(g == pl.num_programs(1) - 1)
    def _():
        lam = _diff_lambda(lam_ref, lam_init)
        of = acc_ref[...] / l_ref[...]
        r = lax.broadcasted_iota(jnp.int32, of.shape, 0)
        head = lax.shift_right_logical(lax.broadcasted_iota(jnp.int32, of.shape, 1), 7)
        pos = jnp.sum(jnp.where(r == 2 * head, of, 0.0), axis=0, keepdims=True)
        neg = jnp.sum(jnp.where(r == 2 * head + 1, of, 0.0), axis=0, keepdims=True)
        o_ref[...] = pos - lam * neg


def _diff_paged(q, k_new, v_new, lam_arr, layer, lam_init, pool_k, pool_v, page_table, pages):
    db, _, w = q.shape
    n_pages = page_table.shape[1]
    assert n_pages % pages == 0
    row = lambda b, g, pt: (b, 0, 0)
    in_specs = [
        pl.BlockSpec((None, 1, w), row),
        pl.BlockSpec((None, 1, w), row),
        pl.BlockSpec((None, 1, w), row),
        pl.BlockSpec((None,) + lam_arr.shape[1:], lambda b, g, pt: (layer, 0, 0)),
    ]
    for k in range(pages):
        in_specs.append(pl.BlockSpec((None, None, w, PAGE_SIZE),
                                     lambda b, g, pt, k=k: (layer, pt[b, g * pages + k], 0, 0)))
    for k in range(pages):
        in_specs.append(pl.BlockSpec((None, None, PAGE_SIZE, DIFF_HEADS, LANES),
                                     lambda b, g, pt, k=k: (layer, pt[b, g * pages + k], 0, 0, 0)))
    comps = 2 * DIFF_HEADS
    return pl.pallas_call(
        functools.partial(_diff_paged_kernel, pages=pages, heads=DIFF_HEADS, lam_init=lam_init),
        grid_spec=pltpu.PrefetchScalarGridSpec(
            num_scalar_prefetch=1,
            grid=(db, n_pages // pages),
            in_specs=in_specs,
            out_specs=pl.BlockSpec((None, 1, w), row),
            scratch_shapes=[
                pltpu.VMEM((comps, w), BF16),
                pltpu.VMEM((comps, 1), F32),
                pltpu.VMEM((comps, 1), F32),
                pltpu.VMEM((comps, w), F32),
            ],
        ),
        out_shape=jax.ShapeDtypeStruct((db, 1, w), F32),
        compiler_params=_cparams(("parallel", "arbitrary")),
        name="diff_paged",
    )(page_table, q, k_new, v_new, lam_arr, *([pool_k] * pages), *([pool_v] * pages))


def _norm_kernel(x_ref, g_ref, o_ref):
    o_ref[...] = _rms(x_ref[...], g_ref[...])


def _final_norm(x, g, tm):
    m, d = x.shape
    return pl.pallas_call(
        _norm_kernel,
        grid=(m // tm,),
        in_specs=[pl.BlockSpec((tm, d), lambda i: (i, 0)), pl.BlockSpec((1, d), lambda i: (0, 0))],
        out_specs=pl.BlockSpec((tm, d), lambda i: (i, 0)),
        out_shape=jax.ShapeDtypeStruct((m, d), F32),
        compiler_params=_cparams(("parallel",)),
        name="final_norm",
    )(x, g.reshape(1, d))


def _mla_weights(w_in, w_q_up, w_kv_up):
    n, q_lora = w_q_up.shape[0], w_q_up.shape[1]
    kv_lora = w_kv_up.shape[1]
    win = jnp.concatenate([w_in, w_in[:, :, -MLA_ROPE:]], axis=-1).astype(BF16)
    wq = w_q_up.reshape(n, q_lora, MLA_HEADS, MLA_NOPE + MLA_ROPE)
    pad = jnp.zeros((n, q_lora, MLA_HEADS, MLA_SLOT - MLA_NOPE - MLA_ROPE), w_q_up.dtype)
    wq = jnp.concatenate([wq, pad], axis=-1).reshape(n, q_lora, MLA_HEADS * MLA_SLOT).astype(BF16)
    wkv = w_kv_up.reshape(n, kv_lora, MLA_HEADS, MLA_NOPE + MLA_V)
    wkv = jnp.concatenate([wkv[..., :MLA_NOPE].reshape(n, kv_lora, -1),
                           wkv[..., MLA_NOPE:].reshape(n, kv_lora, -1)], axis=-1).astype(BF16)
    return win, wq, wkv


def kernel(x_prompt, x_sample, cache_mla, cache_dil_g0, cache_dil_g1, cache_dil_g2, cache_diff_k, cache_diff_v, cache_mem_kv, page_table, mem_prompt, norm_g, mem_norm_g, final_norm_g, ffn_w_gu, ffn_w_down, x_wq, x_wkv, x_wo, mla_w_in, mla_q_norm_g, mla_w_q_up, mla_kv_norm_g, mla_w_kv_up, mla_w_out, dil_w_in, dil_w_out, diff_w_in, diff_lambda, diff_subln_g, diff_w_out):
    bp, sp, d = x_prompt.shape
    bs, ts, _ = x_sample.shape
    assert ts == 1, "sample path handles one new token per sequence"
    depth = norm_g.shape[0]
    past_len = page_table.shape[1] * PAGE_SIZE
    n_pg = sp // PAGE_SIZE
    mem_len = mem_prompt.shape[1]
    n_groups = len(DIL_GROUPS)
    dil_bufs = (cache_dil_g0, cache_dil_g1, cache_dil_g2)
    hw = DIL_HEADS * LANES

    tm_p = 512
    tm_s = bs
    tq = 512
    nrb_p = sp // tm_p

    wgu, wdn = ffn_w_gu.astype(BF16), ffn_w_down.astype(BF16)
    wq_x, wkv_x, wo_x = x_wq.astype(BF16), x_wkv.astype(BF16), x_wo.astype(BF16)
    mla_win, mla_wq, mla_wkv = _mla_weights(mla_w_in, mla_w_q_up, mla_w_kv_up)
    mla_wo = mla_w_out.astype(BF16)
    dil_win, dil_wo = dil_w_in.astype(BF16), dil_w_out.astype(BF16)
    diff_win, diff_wo = diff_w_in.astype(BF16), diff_w_out.astype(BF16)

    g_arr = norm_g.reshape(depth * 4, 1, d)
    mem_g = mem_norm_g.reshape(depth, 1, d)
    mla_qg = mla_q_norm_g.reshape(mla_q_norm_g.shape[0], 1, -1)
    mla_kvg = mla_kv_norm_g.reshape(mla_kv_norm_g.shape[0], 1, -1)
    subln_g = diff_subln_g.reshape(diff_subln_g.shape[0], 1, -1)

    pos_p = jnp.arange(sp, dtype=jnp.int32)
    pos_s = jnp.full((bs,), past_len, dtype=jnp.int32)
    rope_p = {dim: _rope_tables(pos_p, dim) for dim in (64, 128)}
    rope_s = {dim: _rope_tables(pos_s, dim) for dim in (64, 128)}

    cache_mem = cache_mem_kv.reshape(depth, bs, mem_len, -1)
    mem2d = mem_prompt.reshape(bp * mem_len, d)
    pool_k = jnp.transpose(cache_diff_k, (0, 1, 3, 4, 2)).reshape(
        cache_diff_k.shape[0], cache_diff_k.shape[1], -1, PAGE_SIZE)
    pool_v = cache_diff_v
    cache_mla_t = jnp.swapaxes(cache_mla, 2, 3)

    xp = x_prompt.reshape(bp * sp, d)
    xs = x_sample.reshape(bs, d)

    mla_p, mla_s, diff_kp, diff_ks, diff_vp, diff_vs, mem_p = [], [], [], [], [], [], []
    dil_p = [[] for _ in DIL_GROUPS]
    dil_s = [[] for _ in DIL_GROUPS]

    for i in range(depth):
        kind, j = i % N_MIXERS, i // N_MIXERS
        xp = _ffn(xp, g_arr, 4 * i, wgu, wdn, i, 0, tm_p)
        xs = _ffn(xs, g_arr, 4 * i, wgu, wdn, i, 0, tm_s)

        if kind == 0:
            cos, sin = rope_p[MLA_ROPE]
            q, row, krp, kv = _mla_proj(xp, g_arr, 4 * i + 1, mla_win, mla_qg, mla_wq, mla_kvg, mla_wkv,
                                        j, cos, sin, nrb_p, tm_p)
            o_p = _mla_attn(q, kv, krp, bp, sp, tq)
            mla_p.append(row.reshape(bp, n_pg, PAGE_SIZE, -1))
            xp = _out_proj(xp, mla_wo, j, [o_p], tm_p)

            cos, sin = rope_s[MLA_ROPE]
            q, row, _, _ = _mla_proj(xs, g_arr, 4 * i + 1, mla_win, mla_qg, mla_wq, mla_kvg, mla_wkv,
                                     j, cos, sin, 1, tm_s)
            q_abs = _mla_absorb_q(q, mla_wkv, j).reshape(bs, MLA_HEADS, -1)
            o_lat = _mla_paged(q_abs, row.reshape(bs, 1, -1), cache_mla_t, j, page_table, 16)
            o_s = _mla_absorb_o(o_lat.reshape(bs, -1), mla_wkv, j)
            mla_s.append(row.reshape(bs, ts, -1))
            xs = _out_proj(xs, mla_wo, j, [o_s], tm_s)
        elif kind == 1:
            scale = LANES ** -0.5
            os_p, lses_p, qs_s, kvs_s = [], [], [], []
            for g, (w, dd) in enumerate(DIL_GROUPS):
                cfg = [(0, LANES, scale, 0), (1, LANES, 1.0, 0), (1, 0, 1.0, DIL_HEADS)]
                outs = [(DIL_HEADS, F32, None), (2 * DIL_HEADS, F32, None)]
                q, kv = _proj(xp, g_arr, 4 * i + 1, dil_win, j, 3 * g, cfg, outs, tm_p, hw,
                              rope=(*rope_p[LANES], nrb_p))
                o, lse = _dil_attn(q, kv, bp, sp, w, dd)
                os_p.append(o)
                lses_p.append(lse)
                wl = min(w, sp)
                kv_tail = kv.reshape(2, DIL_HEADS, bp, sp, LANES)[:, :, :, sp - wl:]
                dil_p[g].append(jnp.transpose(kv_tail, (2, 3, 0, 1, 4)))
                q, kv = _proj(xs, g_arr, 4 * i + 1, dil_win, j, 3 * g, cfg, outs, tm_s, hw,
                              rope=(*rope_s[LANES], 1))
                qs_s.append(jnp.swapaxes(q, 0, 1))
                kv = jnp.swapaxes(kv, 0, 1)
                kvs_s.append(kv)
                dil_s[g].append(kv.reshape(bs, ts, 2, DIL_HEADS, LANES))
            xp = _out_proj(xp, dil_wo, j, os_p + lses_p, tm_p, mode="dil")
            q_all = jnp.stack(qs_s, axis=1)
            kv_all = jnp.concatenate(kvs_s, axis=1).reshape(bs, 2 * n_groups, DIL_HEADS, LANES)
            o_s = _dil_sample(q_all, kv_all, dil_bufs, j)
            xs = _out_proj(xs, dil_wo, j, [o_s.reshape(bs, -1)], tm_s)
        else:
            lam_init = 0.8 - 0.6 * math.exp(-0.3 * i)
            dw = diff_win.shape[-1] // 3
            scale = (LANES // 2) ** -0.5
            cfg = [(0, LANES // 2, scale, 0), (1, LANES // 2, 1.0, 0), (2, 0, 1.0, 0)]
            outs = [(dw, BF16, lambda jj: 0), (dw, F32, lambda jj: 0), (dw, F32, lambda jj: 0)]
            q, k, v = _proj(xp, g_arr, 4 * i + 1, diff_win, j, 0, cfg, outs, tm_p, dw,
                            rope=(*rope_p[LANES // 2], nrb_p))
            o_p = _diff_attn(q, k, v, diff_lambda, j, lam_init, bp, sp, tq)
            diff_kp.append(k.reshape(bp, n_pg, PAGE_SIZE, 2 * DIFF_HEADS, LANES // 2))
            diff_vp.append(v.reshape(bp, n_pg, PAGE_SIZE, DIFF_HEADS, LANES))
            xp = _out_proj(xp, diff_wo, j, [o_p], tm_p, mode="diff", gain=(subln_g, j),
                           gain_scale=1.0 - lam_init)

            q, k, v = _proj(xs, g_arr, 4 * i + 1, diff_win, j, 0, cfg, [(dw, F32, outs[0][2])] + outs[1:],
                            tm_s, dw, rope=(*rope_s[LANES // 2], 1))
            o_s = _diff_paged(q.reshape(bs, 1, dw), k.reshape(bs, 1, dw), v.reshape(bs, 1, dw),
                              diff_lambda, j, lam_init, pool_k, pool_v, page_table, 8)
            diff_ks.append(k.reshape(bs, ts, 2 * DIFF_HEADS, LANES // 2))
            diff_vs.append(v.reshape(bs, ts, DIFF_HEADS, LANES))
            xs = _out_proj(xs, diff_wo, j, [o_s.reshape(bs, dw)], tm_s, mode="diff", gain=(subln_g, j),
                           gain_scale=1.0 - lam_init)

        kvw = wkv_x.shape[-1]
        (mkv,) = _proj(mem2d, mem_g, i, wkv_x, i, 0, [(0, 0, 1.0, 0)] * (kvw // d),
                       [(kvw, F32, lambda jj: jj)], min(tm_p, mem2d.shape[0]), d)
        mem_p.append(mkv.reshape(bp, mem_len, 2, X_HEADS, d // X_HEADS))
        xp = _cross_prompt(xp, g_arr, 4 * i + 2, wq_x, wo_x, i, mkv.reshape(bp, mem_len, kvw), sp, tm_p)
        (q,) = _proj(xs, g_arr, 4 * i + 2, wq_x, i, 0, [(0, 0, (d // X_HEADS) ** -0.5, 0)],
                     [(d, F32, lambda jj: 0)], tm_s, d)
        o_s = _cross_sample(q.reshape(bs, 1, d), cache_mem, i)
        xs = _out_proj(xs, wo_x, i, [o_s.reshape(bs, d)], tm_s)

        xp = _ffn(xp, g_arr, 4 * i + 3, wgu, wdn, i, 1, tm_p)
        xs = _ffn(xs, g_arr, 4 * i + 3, wgu, wdn, i, 1, tm_s)

    yp = _final_norm(xp, final_norm_g, tm_p).reshape(bp, sp, d)
    ys = _final_norm(xs, final_norm_g, tm_s).reshape(bs, ts, d)
    return (yp, ys,
            jnp.stack(mla_p), jnp.stack(mla_s),
            jnp.stack(dil_p[0]), jnp.stack(dil_s[0]),
            jnp.stack(dil_p[1]), jnp.stack(dil_s[1]),
            jnp.stack(dil_p[2]), jnp.stack(dil_s[2]),
            jnp.stack(diff_kp), jnp.stack(diff_ks),
            jnp.stack(diff_vp), jnp.stack(diff_vs),
            jnp.stack(mem_p))
```

```python
import functools
import math

import jax
import jax.numpy as jnp
from jax import lax
from jax.experimental import pallas as pl
from jax.experimental.pallas import tpu as pltpu

F32 = jnp.float32
BF16 = jnp.bfloat16

NORM_EPS = 1e-6
NEG_INF = -1e30
ROPE_THETA = 10000.0
PAGE_SIZE = 128
N_MIXERS = 3

MLA_HEADS = 8
MLA_NOPE = 128
MLA_ROPE = 64
MLA_V = 128
MLA_SLOT = 256

DIL_GROUPS = ((128, 1), (512, 4), (2048, 16))
DIL_HEADS = 8
DIL_BLOCK = 128

DIFF_HEADS = 8
X_HEADS = 4

LANES = 128
QROWS = 16
VMEM_LIMIT = 48 * 1024 * 1024
FFN_VMEM_LIMIT = 56 * 1024 * 1024
FFN_SUB = 256
MLA_PAGES_PER_STEP = 32
DIFF_PAGES_PER_STEP = 8


def _cparams(sem, vmem=VMEM_LIMIT):
    return pltpu.CompilerParams(dimension_semantics=sem, vmem_limit_bytes=vmem)


def _rms(x, g):
    ms = jnp.mean(x * x, axis=-1, keepdims=True)
    return x * lax.rsqrt(ms + NORM_EPS) * g


def _nt_dot(a, b):
    return lax.dot_general(a, b, (((1,), (1,)), ((), ())), preferred_element_type=F32)


def _rope_tables(pos, dim):
    half = dim // 2
    inv = jnp.exp(jnp.arange(half, dtype=F32) * (-2.0 * math.log(ROPE_THETA) / dim))
    ang = pos.astype(F32)[:, None] * inv[None, :]
    cos, sin = jnp.cos(ang), jnp.sin(ang)
    reps = LANES // dim
    cos_t = jnp.tile(jnp.concatenate([cos, cos], axis=-1), (1, reps))
    sin_t = jnp.tile(jnp.concatenate([-sin, sin], axis=-1), (1, reps))
    return cos_t, sin_t


def _apply_rope(chunk, cos, sin, dim):
    half = dim // 2
    if dim == LANES:
        partner = pltpu.roll(chunk, half, 1)
    else:
        lane = lax.broadcasted_iota(jnp.int32, chunk.shape, 1)
        first = (lane & (dim - 1)) < half
        partner = jnp.where(first, pltpu.roll(chunk, LANES - half, 1), pltpu.roll(chunk, half, 1))
    return chunk * cos + partner * sin


def _ffn_kernel(x_ref, g_ref, wg_ref, wu_ref, wd_ref, o_ref, a_ref):
    x = x_ref[...]
    h = _rms(x, g_ref[...]).astype(BF16)
    f_dim = wg_ref.shape[1]
    for lo in range(0, f_dim, FFN_SUB):
        hi = min(lo + FFN_SUB, f_dim)
        g = jnp.dot(h, wg_ref[:, lo:hi], preferred_element_type=F32)
        u = jnp.dot(h, wu_ref[:, lo:hi], preferred_element_type=F32)
        a_ref[:, lo:hi] = (g * jax.nn.sigmoid(g) * u).astype(BF16)
    o_ref[...] = x + 0.5 * jnp.dot(a_ref[...], wd_ref[...], preferred_element_type=F32)


def _resident(block_shape, index_map):
    return pl.BlockSpec(block_shape, index_map, pipeline_mode=pl.Buffered(1))


def _ffn(x, g_arr, gidx, wgu, wd, layer, which, tm):
    m, d = x.shape
    f_dim = wd.shape[2]
    return pl.pallas_call(
        _ffn_kernel,
        grid=(m // tm,),
        in_specs=[
            pl.BlockSpec((tm, d), lambda i: (i, 0)),
            pl.BlockSpec((None, 1, d), lambda i: (gidx, 0, 0)),
            _resident((None, None, d, f_dim), lambda i: (layer, which, 0, 0)),
            _resident((None, None, d, f_dim), lambda i: (layer, which, 0, 1)),
            _resident((None, None, f_dim, d), lambda i: (layer, which, 0, 0)),
        ],
        out_specs=pl.BlockSpec((tm, d), lambda i: (i, 0)),
        out_shape=jax.ShapeDtypeStruct((m, d), F32),
        scratch_shapes=[pltpu.VMEM((tm, f_dim), BF16)],
        compiler_params=_cparams(("parallel",), FFN_VMEM_LIMIT),
        name="ffn",
    )(x, g_arr, wgu, wgu, wd)


def _proj_kernel(*refs, cfg, n_out, has_rope, tn):
    x_ref, g_ref, w_ref = refs[:3]
    k = 3
    if has_rope:
        cos_ref, sin_ref = refs[3:5]
        k = 5
    outs = refs[k:k + n_out]
    h = _rms(x_ref[...], g_ref[...]).astype(BF16)
    if has_rope:
        cos, sin = cos_ref[...], sin_ref[...]
    for jj, (oi, rdim, scale, slot) in enumerate(cfg):
        acc = jnp.dot(h, w_ref[:, jj * tn:(jj + 1) * tn], preferred_element_type=F32)
        o = outs[oi]
        head_major = len(o.shape) == 3
        for c in range(tn // LANES):
            ch = acc[:, c * LANES:(c + 1) * LANES]
            if rdim:
                ch = _apply_rope(ch, cos, sin, rdim)
            if scale != 1.0:
                ch = ch * scale
            if head_major:
                o[slot + c] = ch.astype(o.dtype)
            else:
                o[:, slot * tn + c * LANES:slot * tn + (c + 1) * LANES] = ch.astype(o.dtype)


def _proj(x, g_arr, gidx, w, wlayer, wblock, cfg, out_defs, tm, tn, rope=None):
    m, kdim = x.shape
    nj = len(cfg)
    in_specs = [
        pl.BlockSpec((tm, kdim), lambda i: (i, 0)),
        pl.BlockSpec((None, 1, kdim), lambda i: (gidx, 0, 0)),
        _resident((None, kdim, nj * tn), lambda i: (wlayer, 0, wblock)),
    ]
    args = [x, g_arr, w]
    if rope is not None:
        cos, sin, nrb = rope
        in_specs += [pl.BlockSpec((tm, LANES), lambda i: (i % nrb, 0))] * 2
        args += [cos, sin]
    out_specs, out_shape = [], []
    for cols, dtype, head_major in out_defs:
        if head_major:
            out_specs.append(pl.BlockSpec((cols, tm, LANES), lambda i: (0, i, 0)))
            out_shape.append(jax.ShapeDtypeStruct((cols, m, LANES), dtype))
        else:
            out_specs.append(pl.BlockSpec((tm, cols), lambda i: (i, 0)))
            out_shape.append(jax.ShapeDtypeStruct((m, cols), dtype))
    return pl.pallas_call(
        functools.partial(_proj_kernel, cfg=tuple(cfg), n_out=len(out_defs), has_rope=rope is not None, tn=tn),
        grid=(m // tm,),
        in_specs=in_specs,
        out_specs=out_specs,
        out_shape=out_shape,
        compiler_params=_cparams(("parallel",)),
        name="proj",
    )(*args)


def _mla_proj_kernel(x_ref, g_ref, win_ref, qg_ref, wq_ref, kvg_ref, wkv_ref, cos_ref, sin_ref,
                     q_ref, row_ref, krp_ref, kv_ref, *, q_lora, kv_lora, scale):
    h = _rms(x_ref[...], g_ref[...]).astype(BF16)
    p = jnp.dot(h, win_ref[...], preferred_element_type=F32)
    cq = p[:, :q_lora]
    ckv = p[:, q_lora:q_lora + kv_lora]
    kr2 = p[:, q_lora + kv_lora:]
    cos, sin = cos_ref[...], sin_ref[...]

    qn = _rms(cq, qg_ref[...]).astype(BF16)
    q = jnp.dot(qn, wq_ref[...], preferred_element_type=F32)
    for c in range(q.shape[1] // LANES):
        ch = q[:, c * LANES:(c + 1) * LANES]
        if c % 2 == 1:
            ch = _apply_rope(ch, cos, sin, MLA_ROPE)
        q_ref[:, c * LANES:(c + 1) * LANES] = (ch * scale).astype(q_ref.dtype)

    c_n = _rms(ckv, kvg_ref[...])
    kr_rot = _apply_rope(kr2, cos, sin, MLA_ROPE)
    row_ref[:, :kv_lora] = c_n
    row_ref[:, kv_lora:] = kr_rot[:, :MLA_ROPE]
    lane = lax.broadcasted_iota(jnp.int32, kr_rot.shape, 1)
    krp_ref[...] = jnp.where(lane < MLA_ROPE, kr_rot, 0.0).astype(krp_ref.dtype)
    kv_ref[...] = jnp.dot(c_n.astype(BF16), wkv_ref[...], preferred_element_type=F32).astype(kv_ref.dtype)


def _mla_proj(x, g_arr, gidx, win, qg, wq, kvg, wkv, layer, cos, sin, nrb, tm):
    m, d = x.shape
    q_lora, kv_lora = qg.shape[-1], kvg.shape[-1]
    n_in = win.shape[-1]
    nq = wq.shape[-1]
    nkv = wkv.shape[-1]
    scale = (MLA_NOPE + MLA_ROPE) ** -0.5
    return pl.pallas_call(
        functools.partial(_mla_proj_kernel, q_lora=q_lora, kv_lora=kv_lora, scale=scale),
        grid=(m // tm,),
        in_specs=[
            pl.BlockSpec((tm, d), lambda i: (i, 0)),
            pl.BlockSpec((None, 1, d), lambda i: (gidx, 0, 0)),
            pl.BlockSpec((None, d, n_in), lambda i: (layer, 0, 0)),
            pl.BlockSpec((None, 1, q_lora), lambda i: (layer, 0, 0)),
            pl.BlockSpec((None, q_lora, nq), lambda i: (layer, 0, 0)),
            pl.BlockSpec((None, 1, kv_lora), lambda i: (layer, 0, 0)),
            pl.BlockSpec((None, kv_lora, nkv), lambda i: (layer, 0, 0)),
            pl.BlockSpec((tm, LANES), lambda i: (i % nrb, 0)),
            pl.BlockSpec((tm, LANES), lambda i: (i % nrb, 0)),
        ],
        out_specs=[
            pl.BlockSpec((tm, nq), lambda i: (i, 0)),
            pl.BlockSpec((tm, kv_lora + MLA_ROPE), lambda i: (i, 0)),
            pl.BlockSpec((tm, LANES), lambda i: (i, 0)),
            pl.BlockSpec((tm, nkv), lambda i: (i, 0)),
        ],
        out_shape=[
            jax.ShapeDtypeStruct((m, nq), BF16),
            jax.ShapeDtypeStruct((m, kv_lora + MLA_ROPE), F32),
            jax.ShapeDtypeStruct((m, LANES), BF16),
            jax.ShapeDtypeStruct((m, nkv), BF16),
        ],
        compiler_params=_cparams(("parallel",)),
        name="mla_proj",
    )(x, g_arr, win, qg, wq, kvg, wkv, cos, sin)


def _flash_update(s, v, m_ref, l_ref, acc_ref, idx, lo, hi):
    m_prev = m_ref[idx]
    m_new = jnp.maximum(m_prev, jnp.max(s, axis=-1, keepdims=True))
    alpha = jnp.exp(m_prev - m_new)
    p = jnp.exp(s - jnp.tile(m_new, (1, s.shape[1] // LANES)))
    l_ref[idx] = alpha * l_ref[idx] + jnp.sum(p, axis=-1, keepdims=True)
    acc_ref[:, lo:hi] = alpha * acc_ref[:, lo:hi] + jnp.dot(p.astype(BF16), v, preferred_element_type=F32)
    m_ref[idx] = m_new


def _causal_mask(tq, tk):
    row = lax.broadcasted_iota(jnp.int32, (tq, tk), 0)
    col = lax.broadcasted_iota(jnp.int32, (tq, tk), 1)
    return row >= col


def _mla_attn_kernel(q_ref, kn_ref, v_ref, krp_ref, o_ref, m_ref, l_ref, acc_ref, *, heads, nk):
    qi, ki = pl.program_id(1), pl.program_id(2)

    @pl.when(ki == 0)
    def _():
        m_ref[...] = jnp.full_like(m_ref, NEG_INF)
        l_ref[...] = jnp.zeros_like(l_ref)
        acc_ref[...] = jnp.zeros_like(acc_ref)

    def step(diagonal):
        mask = _causal_mask(q_ref.shape[0], kn_ref.shape[0]) if diagonal else None
        krp = krp_ref[...]
        for h in range(heads):
            qh = q_ref[:, h * MLA_SLOT:(h + 1) * MLA_SLOT]
            kh = jnp.concatenate([kn_ref[:, h * MLA_NOPE:(h + 1) * MLA_NOPE], krp], axis=-1)
            s = _nt_dot(qh, kh)
            if diagonal:
                s = jnp.where(mask, s, NEG_INF)
            _flash_update(s, v_ref[:, h * MLA_V:(h + 1) * MLA_V], m_ref, l_ref, acc_ref,
                          h, h * MLA_V, (h + 1) * MLA_V)

    pl.when(ki < qi)(functools.partial(step, False))
    pl.when(ki == qi)(functools.partial(step, True))

    @pl.when(ki == nk - 1)
    def _():
        for h in range(heads):
            lo, hi = h * MLA_V, (h + 1) * MLA_V
            o_ref[:, lo:hi] = (acc_ref[:, lo:hi] / l_ref[h]).astype(o_ref.dtype)


def _mla_attn(q, kv, krp, batch, seq, tq):
    m = q.shape[0]
    nq = seq // tq
    hv = MLA_HEADS * MLA_V
    kblk = lambda b, qi, ki: b * nq + jnp.minimum(ki, qi)
    return pl.pallas_call(
        functools.partial(_mla_attn_kernel, heads=MLA_HEADS, nk=nq),
        grid=(batch, nq, nq),
        in_specs=[
            pl.BlockSpec((tq, MLA_HEADS * MLA_SLOT), lambda b, qi, ki: (b * nq + qi, 0)),
            pl.BlockSpec((tq, MLA_HEADS * MLA_NOPE), lambda b, qi, ki: (kblk(b, qi, ki), 0)),
            pl.BlockSpec((tq, hv), lambda b, qi, ki: (kblk(b, qi, ki), 1)),
            pl.BlockSpec((tq, LANES), lambda b, qi, ki: (kblk(b, qi, ki), 0)),
        ],
        out_specs=pl.BlockSpec((tq, hv), lambda b, qi, ki: (b * nq + qi, 0)),
        out_shape=jax.ShapeDtypeStruct((m, hv), BF16),
        scratch_shapes=[
            pltpu.VMEM((MLA_HEADS, tq, LANES), F32),
            pltpu.VMEM((MLA_HEADS, tq, LANES), F32),
            pltpu.VMEM((tq, hv), F32),
        ],
        compiler_params=_cparams(("parallel", "parallel", "arbitrary")),
        name="mla_attn",
    )(q, kv, kv, krp)


def _diff_lambda(lam_ref, lam_init):
    lp = lam_ref[...]
    a = jnp.sum(lp[0:1] * lp[1:2], axis=-1, keepdims=True)
    b = jnp.sum(lp[2:3] * lp[3:4], axis=-1, keepdims=True)
    return jnp.exp(a) - jnp.exp(b) + lam_init


def _diff_attn_kernel(q_ref, k_ref, v_ref, lam_ref, o_ref, m_ref, l_ref, acc0_ref, acc1_ref,
                      *, heads, nk, lam_init):
    qi, ki = pl.program_id(1), pl.program_id(2)
    hd2 = LANES

    @pl.when(ki == 0)
    def _():
        m_ref[...] = jnp.full_like(m_ref, NEG_INF)
        l_ref[...] = jnp.zeros_like(l_ref)
        acc0_ref[...] = jnp.zeros_like(acc0_ref)
        acc1_ref[...] = jnp.zeros_like(acc1_ref)

    def step(diagonal):
        tq = q_ref.shape[0]
        mask = _causal_mask(tq, k_ref.shape[0]) if diagonal else None
        lane = lax.broadcasted_iota(jnp.int32, (tq, hd2), 1)
        for h in range(heads):
            lo, hi = h * hd2, (h + 1) * hd2
            qp = q_ref[:, lo:hi].astype(F32)
            kp = k_ref[:, lo:hi].astype(BF16)
            vh = v_ref[:, lo:hi].astype(BF16)
            for c, acc_ref in ((0, acc0_ref), (1, acc1_ref)):
                sel = (lane < hd2 // 2) if c == 0 else (lane >= hd2 // 2)
                qc = jnp.where(sel, qp, 0.0).astype(BF16)
                s = _nt_dot(qc, kp)
                if diagonal:
                    s = jnp.where(mask, s, NEG_INF)
                _flash_update(s, vh, m_ref, l_ref, acc_ref, 2 * h + c, lo, hi)

    pl.when(ki < qi)(functools.partial(step, False))
    pl.when(ki == qi)(functools.partial(step, True))

    @pl.when(ki == nk - 1)
    def _():
        lam = _diff_lambda(lam_ref, lam_init)
        for h in range(heads):
            lo, hi = h * hd2, (h + 1) * hd2
            o_ref[:, lo:hi] = (acc0_ref[:, lo:hi] / l_ref[2 * h]
                               - lam * (acc1_ref[:, lo:hi] / l_ref[2 * h + 1]))


def _diff_attn(q, k, v, lam_arr, layer, lam_init, batch, seq, tq):
    m, w = q.shape
    nq = seq // tq
    kblk = lambda b, qi, ki: (b * nq + jnp.minimum(ki, qi), 0)
    return pl.pallas_call(
        functools.partial(_diff_attn_kernel, heads=DIFF_HEADS, nk=nq, lam_init=lam_init),
        grid=(batch, nq, nq),
        in_specs=[
            pl.BlockSpec((tq, w), lambda b, qi, ki: (b * nq + qi, 0)),
            pl.BlockSpec((tq, w), kblk),
            pl.BlockSpec((tq, w), kblk),
            pl.BlockSpec((None,) + lam_arr.shape[1:], lambda b, qi, ki: (layer, 0, 0)),
        ],
        out_specs=pl.BlockSpec((tq, w), lambda b, qi, ki: (b * nq + qi, 0)),
        out_shape=jax.ShapeDtypeStruct((m, w), F32),
        scratch_shapes=[
            pltpu.VMEM((2 * DIFF_HEADS, tq, LANES), F32),
            pltpu.VMEM((2 * DIFF_HEADS, tq, LANES), F32),
            pltpu.VMEM((tq, w), F32),
            pltpu.VMEM((tq, w), F32),
        ],
        compiler_params=_cparams(("parallel", "parallel", "arbitrary")),
        name="diff_attn",
    )(q, k, v, lam_arr)


def _dil_attn_kernel(*refs, heads, d, nbs, span, has_prev):
    if has_prev:
        q_ref, kc_ref, vc_ref, kp_ref, vp_ref, o_ref, lse_ref = refs
    else:
        q_ref, kc_ref, vc_ref, o_ref, lse_ref = refs
    first = pl.program_id(1) == 0
    nq = DIL_BLOCK
    nkeys = 2 * nq if has_prev else nq
    qi = lax.broadcasted_iota(jnp.int32, (nq, nkeys), 0)
    kj = lax.broadcasted_iota(jnp.int32, (nq, nkeys), 1)
    dist = qi + (nkeys - nq) - kj
    band = (dist >= 0) & (dist <= span)
    rows = lambda r, i: pl.ds(r + d * nq * i, nq, stride=d) if d > 1 else pl.ds(nq * i, nq)
    for r in range(d):
        for i in range(nbs):
            cur = rows(r, i)
            mask = band
            if has_prev and i == 0:
                mask = band & ((kj >= nq) | jnp.logical_not(first))
            for h in range(heads):
                kh, vh = kc_ref[h, cur, :], vc_ref[h, cur, :]
                if has_prev:
                    if i == 0:
                        kprev, vprev = kp_ref[h, rows(r, 0), :], vp_ref[h, rows(r, 0), :]
                    else:
                        kprev, vprev = kc_ref[h, rows(r, i - 1), :], vc_ref[h, rows(r, i - 1), :]
                    kh = jnp.concatenate([kprev, kh], axis=0)
                    vh = jnp.concatenate([vprev, vh], axis=0)
                s = jnp.where(mask, _nt_dot(q_ref[h, cur, :].astype(BF16), kh.astype(BF16)), NEG_INF)
                m = jnp.max(s, axis=-1, keepdims=True)
                p = jnp.where(mask, jnp.exp(s - m), 0.0)
                l = jnp.sum(p, axis=-1, keepdims=True)
                o = jnp.dot(p.astype(BF16), vh.astype(BF16), preferred_element_type=F32)
                o_ref[h, cur, :] = o / l
                lse_ref[h, cur, :] = jnp.broadcast_to(m + jnp.log(l), (nq, LANES))


def _dil_attn(q, kv, batch, seq, w, d):
    heads, m, hd = q.shape
    stride_rows = DIL_BLOCK * d
    n_rb = seq // stride_rows
    has_prev = n_rb > 1
    nbs = min(n_rb, max(1, 512 // stride_rows))
    tb = stride_rows * nbs
    steps = seq // tb
    hc = max(1, min(heads, 32 // (d * nbs)))
    n_hc = heads // hc
    cur = lambda b, n, c: (c, b * steps + n, 0)
    cur_v = lambda b, n, c: (n_hc + c, b * steps + n, 0)
    blk = (hc, tb, hd)
    in_specs = [pl.BlockSpec(blk, cur), pl.BlockSpec(blk, cur), pl.BlockSpec(blk, cur_v)]
    args = [q, kv, kv]
    if has_prev:
        prev = lambda b, n: b * n_rb + jnp.maximum(n * nbs - 1, 0)
        in_specs += [pl.BlockSpec((hc, stride_rows, hd), lambda b, n, c: (c, prev(b, n), 0)),
                     pl.BlockSpec((hc, stride_rows, hd), lambda b, n, c: (n_hc + c, prev(b, n), 0))]
        args += [kv, kv]
    return pl.pallas_call(
        functools.partial(_dil_attn_kernel, heads=hc, d=d, nbs=nbs, span=w // d, has_prev=has_prev),
        grid=(batch, steps, n_hc),
        in_specs=in_specs,
        out_specs=[pl.BlockSpec(blk, cur)] * 2,
        out_shape=[jax.ShapeDtypeStruct(q.shape, F32)] * 2,
        compiler_params=_cparams(("parallel", "parallel", "parallel")),
        name="dil_attn",
    )(*args)


def _combine_groups(os_, lses):
    m = functools.reduce(jnp.maximum, lses)
    es = [jnp.exp(l - m) for l in lses]
    den = functools.reduce(lambda a, b: a + b, es)
    num = functools.reduce(lambda a, b: a + b, [e * o for e, o in zip(es, os_)])
    return num / den


def _out_kernel(*refs, mode, n_groups, heads, gain_scale):
    x_ref, w_ref = refs[:2]
    o_ref = refs[-1]
    ins = refs[2:-1]
    if mode == "plain":
        a = ins[0][...]
    elif mode == "dil":
        parts = []
        for h in range(ins[0].shape[0]):
            parts.append(_combine_groups([r[h] for r in ins[:n_groups]], [r[h] for r in ins[n_groups:]]))
        a = jnp.concatenate(parts, axis=-1)
    else:
        o, g = ins[0], ins[1][...]
        parts = []
        for h in range(heads):
            parts.append(_rms(o[:, h * LANES:(h + 1) * LANES], g) * gain_scale)
        a = jnp.concatenate(parts, axis=-1)
    o_ref[...] = x_ref[...] + jnp.dot(a.astype(BF16), w_ref[...], preferred_element_type=F32)


def _out_proj(x, w, layer, ins, tm, mode="plain", gain=None, gain_scale=1.0):
    m, d = x.shape
    kdim = w.shape[1]
    in_specs = [pl.BlockSpec((tm, d), lambda i: (i, 0)),
                pl.BlockSpec((None, kdim, d), lambda i: (layer, 0, 0))]
    args = [x, w]
    for a in ins:
        if a.ndim == 3:
            in_specs.append(pl.BlockSpec((a.shape[0], tm, a.shape[2]), lambda i: (0, i, 0)))
        else:
            in_specs.append(pl.BlockSpec((tm, a.shape[1]), lambda i: (i, 0)))
        args.append(a)
    if gain is not None:
        garr, gl = gain
        in_specs.append(pl.BlockSpec((None, 1, garr.shape[-1]), lambda i: (gl, 0, 0)))
        args.append(garr)
    return pl.pallas_call(
        functools.partial(_out_kernel, mode=mode, n_groups=len(DIL_GROUPS), heads=DIFF_HEADS,
                          gain_scale=gain_scale),
        grid=(m // tm,),
        in_specs=in_specs,
        out_specs=pl.BlockSpec((tm, d), lambda i: (i, 0)),
        out_shape=jax.ShapeDtypeStruct((m, d), F32),
        compiler_params=_cparams(("parallel",)),
        name="out_proj",
    )(*args)


def _cross_kernel(x_ref, g_ref, wq_ref, mkv_ref, wo_ref, o_ref, *, heads, scale):
    x = x_ref[...]
    h = _rms(x, g_ref[...]).astype(BF16)
    q = (jnp.dot(h, wq_ref[...], preferred_element_type=F32) * scale).astype(BF16)
    hd = q.shape[1] // heads
    hw = heads * hd
    parts = []
    for hh in range(heads):
        k = mkv_ref[:, hh * hd:(hh + 1) * hd].astype(BF16)
        v = mkv_ref[:, hw + hh * hd:hw + (hh + 1) * hd].astype(BF16)
        s = _nt_dot(q[:, hh * hd:(hh + 1) * hd], k)
        m = jnp.max(s, axis=-1, keepdims=True)
        p = jnp.exp(s - m)
        p = p / jnp.sum(p, axis=-1, keepdims=True)
        parts.append(jnp.dot(p.astype(BF16), v, preferred_element_type=F32).astype(BF16))
    o = jnp.concatenate(parts, axis=-1)
    o_ref[...] = x + jnp.dot(o, wo_ref[...], preferred_element_type=F32)


def _cross_prompt(x, g_arr, gidx, wq, wo, layer, mkv, seq, tm):
    m, d = x.shape
    nrb = seq // tm
    mem_len, kvw = mkv.shape[1], mkv.shape[2]
    hd = d // X_HEADS
    return pl.pallas_call(
        functools.partial(_cross_kernel, heads=X_HEADS, scale=hd ** -0.5),
        grid=(m // tm,),
        in_specs=[
            pl.BlockSpec((tm, d), lambda i: (i, 0)),
            pl.BlockSpec((None, 1, d), lambda i: (gidx, 0, 0)),
            pl.BlockSpec((None, d, d), lambda i: (layer, 0, 0)),
            pl.BlockSpec((None, mem_len, kvw), lambda i: (i // nrb, 0, 0)),
            pl.BlockSpec((None, d, d), lambda i: (layer, 0, 0)),
        ],
        out_specs=pl.BlockSpec((tm, d), lambda i: (i, 0)),
        out_shape=jax.ShapeDtypeStruct((m, d), F32),
        compiler_params=_cparams(("parallel",)),
        name="cross_prompt",
    )(x, g_arr, wq, mkv, wo)


def _block_diag(qrow, rows, head_w):
    w = qrow.shape[1]
    r = lax.broadcasted_iota(jnp.int32, (rows, w), 0)
    lane = lax.broadcasted_iota(jnp.int32, (rows, w), 1)
    sel = (lane >= r * head_w) & (lane < (r + 1) * head_w)
    qb = jnp.where(sel, jnp.broadcast_to(qrow.astype(F32), (rows, w)), 0.0)
    return qb.astype(BF16), sel


def _chunk_rows(x, heads, chunks):
    lead = x.shape[:-1]
    x = x.reshape(*lead, heads, chunks, LANES)
    return jnp.swapaxes(x, -3, -2).reshape(*lead, chunks * heads, LANES)


def _unchunk_rows(x, heads, chunks):
    lead = x.shape[:-2]
    x = x.reshape(*lead, chunks, heads, LANES)
    return jnp.swapaxes(x, -3, -2).reshape(*lead, heads * chunks * LANES)


def _cross_sample_kernel(q_ref, mkv_ref, o_ref, *, heads):
    q = _bf16_round(q_ref[...])
    k = _bf16_round(mkv_ref[:, 0])
    v = _bf16_round(mkv_ref[:, 1])
    part = jnp.sum(k * q[None], axis=-1, keepdims=True)
    rows = part.shape[1]
    s = part
    for j in range(1, rows // heads):
        s = s + jnp.concatenate([part[:, j * heads:], part[:, :j * heads]], axis=1)
    m = jnp.max(s, axis=0)
    p = jnp.exp(s - m[None])
    l = jnp.sum(p, axis=0)
    o_ref[...] = jnp.sum(_bf16_round(p) * v, axis=0) / l


def _cross_sample(q, mem_kv, layer):
    db, rows, _ = q.shape
    mem_len = mem_kv.shape[2]
    return pl.pallas_call(
        functools.partial(_cross_sample_kernel, heads=X_HEADS),
        grid=(db,),
        in_specs=[
            pl.BlockSpec((None, rows, LANES), lambda b: (b, 0, 0)),
            pl.BlockSpec((None, None, mem_len, 2, rows, LANES), lambda b: (layer, b, 0, 0, 0, 0)),
        ],
        out_specs=pl.BlockSpec((None, rows, LANES), lambda b: (b, 0, 0)),
        out_shape=jax.ShapeDtypeStruct((db, rows, LANES), F32),
        compiler_params=_cparams(("parallel",)),
        name="cross_sample",
    )(q, mem_kv)


def _bf16_round(x):
    return x.astype(BF16).astype(F32)


def _dil_sample_kernel(*refs, n_groups):
    q_ref, kv_ref = refs[:2]
    bufs = refs[2:2 + n_groups]
    o_ref = refs[2 + n_groups]
    outs, lses = [], []
    for g in range(n_groups):
        q = _bf16_round(q_ref[g])
        k_new = _bf16_round(kv_ref[2 * g])
        v_new = _bf16_round(kv_ref[2 * g + 1])
        k = _bf16_round(bufs[g][:, 0])
        v = _bf16_round(bufs[g][:, 1])
        s = jnp.sum(k * q[None], axis=-1, keepdims=True)
        s_new = jnp.sum(q * k_new, axis=-1, keepdims=True)
        m = jnp.maximum(jnp.max(s, axis=0), s_new)
        p = jnp.exp(s - m[None])
        p_new = jnp.exp(s_new - m)
        l = jnp.sum(p, axis=0) + p_new
        o = jnp.sum(_bf16_round(p) * v, axis=0) + _bf16_round(p_new) * v_new
        outs.append(o / l)
        lses.append(m + jnp.log(l))
    o_ref[...] = _combine_groups(outs, lses)


def _dil_sample(q, kv_new, bufs, layer):
    db, n_groups, heads, hd = q.shape
    in_specs = [
        pl.BlockSpec((None, n_groups, heads, hd), lambda b: (b, 0, 0, 0)),
        pl.BlockSpec((None, 2 * n_groups, heads, hd), lambda b: (b, 0, 0, 0)),
    ]
    args = [q, kv_new]
    for (wg, d), buf in zip(DIL_GROUPS, bufs):
        wb = buf.shape[2]
        assert wb == wg and wb % d == 0, "window buffer must hold exactly the group's window"
        args.append(buf.reshape(buf.shape[0], db, wb // d, d, 2, heads, hd))
        in_specs.append(pl.BlockSpec((None, None, wb // d, None, 2, heads, hd),
                                     lambda b: (layer, b, 0, 0, 0, 0, 0)))
    return pl.pallas_call(
        functools.partial(_dil_sample_kernel, n_groups=n_groups),
        grid=(db,),
        in_specs=in_specs,
        out_specs=pl.BlockSpec((None, heads, hd), lambda b: (b, 0, 0)),
        out_shape=jax.ShapeDtypeStruct((db, heads, hd), F32),
        compiler_params=_cparams(("parallel",)),
        name="dil_sample",
    )(*args)


def _mla_absorb_q_kernel(q_ref, wkv_ref, o_ref, *, heads, kv_lora):
    slot_out = kv_lora + LANES
    for h in range(heads):
        qn = q_ref[:, h * MLA_SLOT:h * MLA_SLOT + MLA_NOPE]
        w_uk = wkv_ref[:, h * MLA_NOPE:(h + 1) * MLA_NOPE]
        o_ref[:, h * slot_out:h * slot_out + kv_lora] = _nt_dot(qn, w_uk).astype(o_ref.dtype)
        o_ref[:, h * slot_out + kv_lora:(h + 1) * slot_out] = (
            q_ref[:, h * MLA_SLOT + MLA_NOPE:(h + 1) * MLA_SLOT].astype(o_ref.dtype))


def _mla_absorb_q(q, wkv, layer):
    db = q.shape[0]
    kv_lora, nkv = wkv.shape[1], wkv.shape[2]
    slot_out = kv_lora + LANES
    return pl.pallas_call(
        functools.partial(_mla_absorb_q_kernel, heads=MLA_HEADS, kv_lora=kv_lora),
        grid=(1,),
        in_specs=[pl.BlockSpec(q.shape, lambda i: (0, 0)),
                  pl.BlockSpec((None, kv_lora, nkv), lambda i: (layer, 0, 0))],
        out_specs=pl.BlockSpec((db, MLA_HEADS * slot_out), lambda i: (0, 0)),
        out_shape=jax.ShapeDtypeStruct((db, MLA_HEADS * slot_out), F32),
        compiler_params=_cparams(("arbitrary",)),
        name="mla_absorb_q",
    )(q, wkv)


def _mla_paged_kernel(*refs, pages, kv_lora, row_w):
    pt_ref, q_ref, new_ref = refs[:3]
    page_refs = refs[3:3 + pages]
    o_ref = refs[3 + pages]
    rows_ref, m_ref, l_ref, acc_ref = refs[4 + pages:]
    del pt_ref
    g = pl.program_id(1)
    kpad = rows_ref.shape[0]

    @pl.when(g == 0)
    def _():
        rows_ref[row_w:, :] = jnp.zeros((kpad - row_w, rows_ref.shape[1]), BF16)
        new = _bf16_round(new_ref[...])
        qf = _bf16_round(q_ref[:, :row_w])
        m_ref[...] = jnp.sum(qf * new, axis=-1, keepdims=True)
        l_ref[...] = jnp.ones_like(l_ref)
        acc_ref[...] = jnp.broadcast_to(new[:, :kv_lora], acc_ref.shape)

    for k in range(pages):
        rows_ref[:row_w, k * PAGE_SIZE:(k + 1) * PAGE_SIZE] = page_refs[k][...].astype(BF16)
    s = jnp.dot(q_ref[...].astype(BF16), rows_ref[...], preferred_element_type=F32)
    m_prev = m_ref[...]
    m_new = jnp.maximum(m_prev, jnp.max(s, axis=-1, keepdims=True))
    alpha = jnp.exp(m_prev - m_new)
    p = jnp.exp(s - m_new)
    l_ref[...] = alpha * l_ref[...] + jnp.sum(p, axis=-1, keepdims=True)
    acc_ref[...] = alpha * acc_ref[...] + _nt_dot(p.astype(BF16), rows_ref[:kv_lora, :])
    m_ref[...] = m_new

    @pl.when(g == pl.num_programs(1) - 1)
    def _():
        o_ref[...] = (acc_ref[...] / l_ref[...]).astype(o_ref.dtype)


def _mla_paged(q_abs, row_new, cache, layer, page_table, pages):
    db, heads, kpad = q_abs.shape
    row_w = cache.shape[2]
    kv_lora = row_w - MLA_ROPE
    n_pages = page_table.shape[1]
    assert n_pages % pages == 0
    in_specs = [
        pl.BlockSpec((None, heads, kpad), lambda b, g, pt: (b, 0, 0)),
        pl.BlockSpec((None, 1, row_w), lambda b, g, pt: (b, 0, 0)),
    ]
    for k in range(pages):
        in_specs.append(pl.BlockSpec((None, None, row_w, PAGE_SIZE),
                                     lambda b, g, pt, k=k: (layer, pt[b, g * pages + k], 0, 0)))
    return pl.pallas_call(
        functools.partial(_mla_paged_kernel, pages=pages, kv_lora=kv_lora, row_w=row_w),
        grid_spec=pltpu.PrefetchScalarGridSpec(
            num_scalar_prefetch=1,
            grid=(db, n_pages // pages),
            in_specs=in_specs,
            out_specs=pl.BlockSpec((None, heads, kv_lora), lambda b, g, pt: (b, 0, 0)),
            scratch_shapes=[
                pltpu.VMEM((kpad, pages * PAGE_SIZE), BF16),
                pltpu.VMEM((heads, 1), F32),
                pltpu.VMEM((heads, 1), F32),
                pltpu.VMEM((heads, kv_lora), F32),
            ],
        ),
        out_shape=jax.ShapeDtypeStruct((db, heads, kv_lora), F32),
        compiler_params=_cparams(("parallel", "arbitrary")),
        name="mla_paged",
    )(page_table, q_abs, row_new, *([cache] * pages))


def _mla_absorb_o_kernel(o_ref, wkv_ref, out_ref, *, heads, kv_lora):
    v0 = heads * MLA_NOPE
    for h in range(heads):
        w_uv = wkv_ref[:, v0 + h * MLA_V:v0 + (h + 1) * MLA_V]
        out_ref[:, h * MLA_V:(h + 1) * MLA_V] = jnp.dot(
            o_ref[:, h * kv_lora:(h + 1) * kv_lora].astype(BF16), w_uv,
            preferred_element_type=F32).astype(out_ref.dtype)


def _mla_absorb_o(o_lat, wkv, layer):
    db = o_lat.shape[0]
    kv_lora, nkv = wkv.shape[1], wkv.shape[2]
    return pl.pallas_call(
        functools.partial(_mla_absorb_o_kernel, heads=MLA_HEADS, kv_lora=kv_lora),
        grid=(1,),
        in_specs=[pl.BlockSpec(o_lat.shape, lambda i: (0, 0)),
                  pl.BlockSpec((None, kv_lora, nkv), lambda i: (layer, 0, 0))],
        out_specs=pl.BlockSpec((db, MLA_HEADS * MLA_V), lambda i: (0, 0)),
        out_shape=jax.ShapeDtypeStruct((db, MLA_HEADS * MLA_V), BF16),
        compiler_params=_cparams(("arbitrary",)),
        name="mla_absorb_o",
    )(o_lat, wkv)


def _diff_paged_kernel(*refs, pages, heads, lam_init):
    pt_ref, q_ref, kn_ref, vn_ref, lam_ref = refs[:5]
    k_refs = refs[5:5 + pages]
    v_refs = refs[5 + pages:5 + 2 * pages]
    o_ref = refs[5 + 2 * pages]
    qb_ref, m_ref, l_ref, acc_ref = refs[6 + 2 * pages:]
    del pt_ref
    g = pl.program_id(1)
    comps = 2 * heads
    half = LANES // 2

    @pl.when(g == 0)
    def _():
        qb, _ = _block_diag(q_ref[...].astype(BF16), comps, half)
        qb_ref[...] = qb
        kn = kn_ref[...].astype(BF16).astype(F32)
        m_ref[...] = jnp.sum(qb.astype(F32) * kn, axis=-1, keepdims=True)
        l_ref[...] = jnp.ones_like(l_ref)
        acc_ref[...] = jnp.broadcast_to(vn_ref[...].astype(BF16).astype(F32), acc_ref.shape)

    qb = qb_ref[...]
    s = jnp.concatenate([jnp.dot(qb, k_refs[k][...].astype(BF16), preferred_element_type=F32)
                         for k in range(pages)], axis=-1)
    m_prev = m_ref[...]
    m_new = jnp.maximum(m_prev, jnp.max(s, axis=-1, keepdims=True))
    alpha = jnp.exp(m_prev - m_new)
    p32 = jnp.exp(s - m_new)
    l_ref[...] = alpha * l_ref[...] + jnp.sum(p32, axis=-1, keepdims=True)
    p = p32.astype(BF16)
    for h in range(heads):
        vh = jnp.concatenate([v_refs[k][:, h, :] for k in range(pages)], axis=0).astype(BF16)
        lo, hi = h * LANES, (h + 1) * LANES
        acc_ref[:, lo:hi] = alpha * acc_ref[:, lo:hi] + jnp.dot(p, vh, preferred_element_type=F32)
    m_ref[...] = m_new

    @pl.when(g == pl.num_programs(1) - 1)
    def _():
        lam = _diff_lambda(lam_ref, lam_init)
        of = acc_ref[...] / l_ref[...]
        r = lax.broadcasted_iota(jnp.int32, of.shape, 0)
        head = lax.shift_right_logical(lax.broadcasted_iota(jnp.int32, of.shape, 1), 7)
        pos = jnp.sum(jnp.where(r == 2 * head, of, 0.0), axis=0, keepdims=True)
        neg = jnp.sum(jnp.where(r == 2 * head + 1, of, 0.0), axis=0, keepdims=True)
        o_ref[...] = pos - lam * neg


def _diff_paged(q, k_new, v_new, lam_arr, layer, lam_init, pool_k, pool_v, page_table, pages):
    db, _, w = q.shape
    n_pages = page_table.shape[1]
    assert n_pages % pages == 0
    row = lambda b, g, pt: (b, 0, 0)
    in_specs = [
        pl.BlockSpec((None, 1, w), row),
        pl.BlockSpec((None, 1, w), row),
        pl.BlockSpec((None, 1, w), row),
        pl.BlockSpec((None,) + lam_arr.shape[1:], lambda b, g, pt: (layer, 0, 0)),
    ]
    for k in range(pages):
        in_specs.append(pl.BlockSpec((None, None, w, PAGE_SIZE),
                                     lambda b, g, pt, k=k: (layer, pt[b, g * pages + k], 0, 0)))
    for k in range(pages):
        in_specs.append(pl.BlockSpec((None, None, PAGE_SIZE, DIFF_HEADS, LANES),
                                     lambda b, g, pt, k=k: (layer, pt[b, g * pages + k], 0, 0, 0)))
    comps = 2 * DIFF_HEADS
    return pl.pallas_call(
        functools.partial(_diff_paged_kernel, pages=pages, heads=DIFF_HEADS, lam_init=lam_init),
        grid_spec=pltpu.PrefetchScalarGridSpec(
            num_scalar_prefetch=1,
            grid=(db, n_pages // pages),
            in_specs=in_specs,
            out_specs=pl.BlockSpec((None, 1, w), row),
            scratch_shapes=[
                pltpu.VMEM((comps, w), BF16),
                pltpu.VMEM((comps, 1), F32),
                pltpu.VMEM((comps, 1), F32),
                pltpu.VMEM((comps, w), F32),
            ],
        ),
        out_shape=jax.ShapeDtypeStruct((db, 1, w), F32),
        compiler_params=_cparams(("parallel", "arbitrary")),
        name="diff_paged",
    )(page_table, q, k_new, v_new, lam_arr, *([pool_k] * pages), *([pool_v] * pages))


def _norm_kernel(x_ref, g_ref, o_ref):
    o_ref[...] = _rms(x_ref[...], g_ref[...])


def _final_norm(x, g, tm):
    m, d = x.shape
    return pl.pallas_call(
        _norm_kernel,
        grid=(m // tm,),
        in_specs=[pl.BlockSpec((tm, d), lambda i: (i, 0)), pl.BlockSpec((1, d), lambda i: (0, 0))],
        out_specs=pl.BlockSpec((tm, d), lambda i: (i, 0)),
        out_shape=jax.ShapeDtypeStruct((m, d), F32),
        compiler_params=_cparams(("parallel",)),
        name="final_norm",
    )(x, g.reshape(1, d))


def _mla_weights(w_in, w_q_up, w_kv_up):
    n, q_lora = w_q_up.shape[0], w_q_up.shape[1]
    kv_lora = w_kv_up.shape[1]
    win = jnp.concatenate([w_in, w_in[:, :, -MLA_ROPE:]], axis=-1).astype(BF16)
    wq = w_q_up.reshape(n, q_lora, MLA_HEADS, MLA_NOPE + MLA_ROPE)
    pad = jnp.zeros((n, q_lora, MLA_HEADS, MLA_SLOT - MLA_NOPE - MLA_ROPE), w_q_up.dtype)
    wq = jnp.concatenate([wq, pad], axis=-1).reshape(n, q_lora, MLA_HEADS * MLA_SLOT).astype(BF16)
    wkv = w_kv_up.reshape(n, kv_lora, MLA_HEADS, MLA_NOPE + MLA_V)
    wkv = jnp.concatenate([wkv[..., :MLA_NOPE].reshape(n, kv_lora, -1),
                           wkv[..., MLA_NOPE:].reshape(n, kv_lora, -1)], axis=-1).astype(BF16)
    return win, wq, wkv


def kernel(x_prompt, x_sample, cache_mla, cache_dil_g0, cache_dil_g1, cache_dil_g2, cache_diff_k, cache_diff_v, cache_mem_kv, page_table, mem_prompt, norm_g, mem_norm_g, final_norm_g, ffn_w_gu, ffn_w_down, x_wq, x_wkv, x_wo, mla_w_in, mla_q_norm_g, mla_w_q_up, mla_kv_norm_g, mla_w_kv_up, mla_w_out, dil_w_in, dil_w_out, diff_w_in, diff_lambda, diff_subln_g, diff_w_out):
    bp, sp, d = x_prompt.shape
    bs, ts, _ = x_sample.shape
    assert ts == 1, "sample path handles one new token per sequence"
    depth = norm_g.shape[0]
    past_len = page_table.shape[1] * PAGE_SIZE
    n_pg = sp // PAGE_SIZE
    mem_len = mem_prompt.shape[1]
    n_groups = len(DIL_GROUPS)
    dil_bufs = (cache_dil_g0, cache_dil_g1, cache_dil_g2)
    hw = DIL_HEADS * LANES

    tm_p = 512
    tm_ffn = 1024
    tm_s = bs
    tq = 512
    nrb_p = sp // tm_p

    wgu, wdn = ffn_w_gu.astype(BF16), ffn_w_down.astype(BF16)
    wq_x, wkv_x, wo_x = x_wq.astype(BF16), x_wkv.astype(BF16), x_wo.astype(BF16)
    mla_win, mla_wq, mla_wkv = _mla_weights(mla_w_in, mla_w_q_up, mla_w_kv_up)
    mla_wo = mla_w_out.astype(BF16)
    dil_win, dil_wo = dil_w_in.astype(BF16), dil_w_out.astype(BF16)
    diff_win, diff_wo = diff_w_in.astype(BF16), diff_w_out.astype(BF16)

    g_arr = norm_g.reshape(depth * 4, 1, d)
    mem_g = mem_norm_g.reshape(depth, 1, d)
    mla_qg = mla_q_norm_g.reshape(mla_q_norm_g.shape[0], 1, -1)
    mla_kvg = mla_kv_norm_g.reshape(mla_kv_norm_g.shape[0], 1, -1)
    subln_g = diff_subln_g.reshape(diff_subln_g.shape[0], 1, -1)

    pos_p = jnp.arange(sp, dtype=jnp.int32)
    pos_s = jnp.full((bs,), past_len, dtype=jnp.int32)
    rope_p = {dim: _rope_tables(pos_p, dim) for dim in (64, 128)}
    rope_s = {dim: _rope_tables(pos_s, dim) for dim in (64, 128)}

    x_chunks = d // X_HEADS // LANES
    cache_mem = _chunk_rows(cache_mem_kv.reshape(depth, bs, mem_len, 2, d), X_HEADS, x_chunks)
    mem2d = mem_prompt.reshape(bp * mem_len, d)
    pool_k = jnp.transpose(cache_diff_k, (0, 1, 3, 4, 2)).reshape(
        cache_diff_k.shape[0], cache_diff_k.shape[1], -1, PAGE_SIZE)
    pool_v = cache_diff_v
    cache_mla_t = jnp.swapaxes(cache_mla, 2, 3)

    xp = x_prompt.reshape(bp * sp, d)
    xs = x_sample.reshape(bs, d)

    mla_p, mla_s, diff_kp, diff_ks, diff_vp, diff_vs, mem_p = [], [], [], [], [], [], []
    dil_p = [[] for _ in DIL_GROUPS]
    dil_s = [[] for _ in DIL_GROUPS]

    for i in range(depth):
        kind, j = i % N_MIXERS, i // N_MIXERS
        xp = _ffn(xp, g_arr, 4 * i, wgu, wdn, i, 0, tm_ffn)
        xs = _ffn(xs, g_arr, 4 * i, wgu, wdn, i, 0, tm_s)

        if kind == 0:
            cos, sin = rope_p[MLA_ROPE]
            q, row, krp, kv = _mla_proj(xp, g_arr, 4 * i + 1, mla_win, mla_qg, mla_wq, mla_kvg, mla_wkv,
                                        j, cos, sin, nrb_p, tm_p)
            o_p = _mla_attn(q, kv, krp, bp, sp, tq)
            mla_p.append(row.reshape(bp, n_pg, PAGE_SIZE, -1))
            xp = _out_proj(xp, mla_wo, j, [o_p], tm_p)

            cos, sin = rope_s[MLA_ROPE]
            q, row, _, _ = _mla_proj(xs, g_arr, 4 * i + 1, mla_win, mla_qg, mla_wq, mla_kvg, mla_wkv,
                                     j, cos, sin, 1, tm_s)
            q_abs = _mla_absorb_q(q, mla_wkv, j).reshape(bs, MLA_HEADS, -1)
            o_lat = _mla_paged(q_abs, row.reshape(bs, 1, -1), cache_mla_t, j, page_table, MLA_PAGES_PER_STEP)
            o_s = _mla_absorb_o(o_lat.reshape(bs, -1), mla_wkv, j)
            mla_s.append(row.reshape(bs, ts, -1))
            xs = _out_proj(xs, mla_wo, j, [o_s], tm_s)
        elif kind == 1:
            scale = LANES ** -0.5
            os_p, lses_p, qs_s, kvs_s = [], [], [], []
            for g, (w, dd) in enumerate(DIL_GROUPS):
                cfg = [(0, LANES, scale, 0), (1, LANES, 1.0, 0), (1, 0, 1.0, DIL_HEADS)]
                outs = [(DIL_HEADS, F32, True), (2 * DIL_HEADS, F32, True)]
                q, kv = _proj(xp, g_arr, 4 * i + 1, dil_win, j, g, cfg, outs, tm_p, hw,
                              rope=(*rope_p[LANES], nrb_p))
                o, lse = _dil_attn(q, kv, bp, sp, w, dd)
                os_p.append(o)
                lses_p.append(lse)
                wl = min(w, sp)
                kv_tail = kv.reshape(2, DIL_HEADS, bp, sp, LANES)[:, :, :, sp - wl:]
                dil_p[g].append(jnp.transpose(kv_tail, (2, 3, 0, 1, 4)))
                q, kv = _proj(xs, g_arr, 4 * i + 1, dil_win, j, g, cfg, outs, tm_s, hw,
                              rope=(*rope_s[LANES], 1))
                qs_s.append(jnp.swapaxes(q, 0, 1))
                kv = jnp.swapaxes(kv, 0, 1)
                kvs_s.append(kv)
                dil_s[g].append(kv.reshape(bs, ts, 2, DIL_HEADS, LANES))
            xp = _out_proj(xp, dil_wo, j, os_p + lses_p, tm_p, mode="dil")
            q_all = jnp.stack(qs_s, axis=1)
            kv_all = jnp.concatenate(kvs_s, axis=1).reshape(bs, 2 * n_groups, DIL_HEADS, LANES)
            o_s = _dil_sample(q_all, kv_all, dil_bufs, j)
            xs = _out_proj(xs, dil_wo, j, [o_s.reshape(bs, -1)], tm_s)
        else:
            lam_init = 0.8 - 0.6 * math.exp(-0.3 * i)
            dw = diff_win.shape[-1] // 3
            scale = (LANES // 2) ** -0.5
            cfg = [(0, LANES // 2, scale, 0), (1, LANES // 2, 1.0, 0), (2, 0, 1.0, 0)]
            outs = [(dw, BF16, False), (dw, F32, False), (dw, F32, False)]
            q, k, v = _proj(xp, g_arr, 4 * i + 1, diff_win, j, 0, cfg, outs, tm_p, dw,
                            rope=(*rope_p[LANES // 2], nrb_p))
            o_p = _diff_attn(q, k, v, diff_lambda, j, lam_init, bp, sp, tq)
            diff_kp.append(k.reshape(bp, n_pg, PAGE_SIZE, 2 * DIFF_HEADS, LANES // 2))
            diff_vp.append(v.reshape(bp, n_pg, PAGE_SIZE, DIFF_HEADS, LANES))
            xp = _out_proj(xp, diff_wo, j, [o_p], tm_p, mode="diff", gain=(subln_g, j),
                           gain_scale=1.0 - lam_init)

            q, k, v = _proj(xs, g_arr, 4 * i + 1, diff_win, j, 0, cfg, [(dw, F32, False)] + outs[1:],
                            tm_s, dw, rope=(*rope_s[LANES // 2], 1))
            o_s = _diff_paged(q.reshape(bs, 1, dw), k.reshape(bs, 1, dw), v.reshape(bs, 1, dw),
                              diff_lambda, j, lam_init, pool_k, pool_v, page_table, DIFF_PAGES_PER_STEP)
            diff_ks.append(k.reshape(bs, ts, 2 * DIFF_HEADS, LANES // 2))
            diff_vs.append(v.reshape(bs, ts, DIFF_HEADS, LANES))
            xs = _out_proj(xs, diff_wo, j, [o_s.reshape(bs, dw)], tm_s, mode="diff", gain=(subln_g, j),
                           gain_scale=1.0 - lam_init)

        kvw = wkv_x.shape[-1]
        (mkv,) = _proj(mem2d, mem_g, i, wkv_x, i, 0, [(0, 0, 1.0, jj) for jj in range(kvw // d)],
                       [(kvw, F32, False)], min(tm_p, mem2d.shape[0]), d)
        mem_p.append(mkv.reshape(bp, mem_len, 2, X_HEADS, d // X_HEADS))
        xp = _cross_prompt(xp, g_arr, 4 * i + 2, wq_x, wo_x, i, mkv.reshape(bp, mem_len, kvw), sp, tm_p)
        (q,) = _proj(xs, g_arr, 4 * i + 2, wq_x, i, 0, [(0, 0, (d // X_HEADS) ** -0.5, 0)],
                     [(d, F32, False)], tm_s, d)
        o_s = _cross_sample(_chunk_rows(q, X_HEADS, x_chunks), cache_mem, i)
        xs = _out_proj(xs, wo_x, i, [_unchunk_rows(o_s, X_HEADS, x_chunks)], tm_s)

        xp = _ffn(xp, g_arr, 4 * i + 3, wgu, wdn, i, 1, tm_ffn)
        xs = _ffn(xs, g_arr, 4 * i + 3, wgu, wdn, i, 1, tm_s)

    yp = _final_norm(xp, final_norm_g, tm_p).reshape(bp, sp, d)
    ys = _final_norm(xs, final_norm_g, tm_s).reshape(bs, ts, d)
    return (yp, ys,
            jnp.stack(mla_p), jnp.stack(mla_s),
            jnp.stack(dil_p[0]), jnp.stack(dil_s[0]),
            jnp.stack(dil_p[1]), jnp.stack(dil_s[1]),
            jnp.stack(dil_p[2]), jnp.stack(dil_s[2]),
            jnp.stack(diff_kp), jnp.stack(diff_ks),
            jnp.stack(diff_vp), jnp.stack(diff_vs),
            jnp.stack(mem_p))
```

```python
import functools
import math

import jax
import jax.numpy as jnp
from jax import lax
from jax.experimental import pallas as pl
from jax.experimental.pallas import tpu as pltpu

F32 = jnp.float32
BF16 = jnp.bfloat16

NORM_EPS = 1e-6
NEG_INF = -1e30
ROPE_THETA = 10000.0
PAGE_SIZE = 128
N_MIXERS = 3

MLA_HEADS = 8
MLA_NOPE = 128
MLA_ROPE = 64
MLA_V = 128
MLA_SLOT = 256

DIL_GROUPS = ((128, 1), (512, 4), (2048, 16))
DIL_HEADS = 8
DIL_BLOCK = 128

DIFF_HEADS = 8
X_HEADS = 4

LANES = 128
QROWS = 16
VMEM_LIMIT = 48 * 1024 * 1024
FFN_VMEM_LIMIT = 56 * 1024 * 1024
FFN_SUB = 256
MLA_PAGES_PER_STEP = 32
DIFF_PAGES_PER_STEP = 8


def _cparams(sem, vmem=VMEM_LIMIT):
    return pltpu.CompilerParams(dimension_semantics=sem, vmem_limit_bytes=vmem)


def _rms(x, g):
    ms = jnp.mean(x * x, axis=-1, keepdims=True)
    return x * lax.rsqrt(ms + NORM_EPS) * g


def _nt_dot(a, b):
    return lax.dot_general(a, b, (((1,), (1,)), ((), ())), preferred_element_type=F32)


def _rope_tables(pos, dim):
    half = dim // 2
    inv = jnp.exp(jnp.arange(half, dtype=F32) * (-2.0 * math.log(ROPE_THETA) / dim))
    ang = pos.astype(F32)[:, None] * inv[None, :]
    cos, sin = jnp.cos(ang), jnp.sin(ang)
    reps = LANES // dim
    cos_t = jnp.tile(jnp.concatenate([cos, cos], axis=-1), (1, reps))
    sin_t = jnp.tile(jnp.concatenate([-sin, sin], axis=-1), (1, reps))
    return cos_t, sin_t


def _apply_rope(chunk, cos, sin, dim):
    half = dim // 2
    if dim == LANES:
        partner = pltpu.roll(chunk, half, 1)
    else:
        lane = lax.broadcasted_iota(jnp.int32, chunk.shape, 1)
        first = (lane & (dim - 1)) < half
        partner = jnp.where(first, pltpu.roll(chunk, LANES - half, 1), pltpu.roll(chunk, half, 1))
    return chunk * cos + partner * sin


def _ffn_kernel(*refs, final_norm):
    if final_norm:
        x_ref, g_ref, wg_ref, wu_ref, wd_ref, fg_ref, o_ref, a_ref = refs
    else:
        x_ref, g_ref, wg_ref, wu_ref, wd_ref, o_ref, a_ref = refs
    x = x_ref[...]
    h = _rms(x, g_ref[...]).astype(BF16)
    f_dim = wg_ref.shape[1]
    for lo in range(0, f_dim, FFN_SUB):
        hi = min(lo + FFN_SUB, f_dim)
        g = jnp.dot(h, wg_ref[:, lo:hi], preferred_element_type=F32)
        u = jnp.dot(h, wu_ref[:, lo:hi], preferred_element_type=F32)
        a_ref[:, lo:hi] = (g * jax.nn.sigmoid(g) * u).astype(BF16)
    y = x + 0.5 * jnp.dot(a_ref[...], wd_ref[...], preferred_element_type=F32)
    o_ref[...] = _rms(y, fg_ref[...]) if final_norm else y


def _resident(block_shape, index_map):
    return pl.BlockSpec(block_shape, index_map, pipeline_mode=pl.Buffered(1))


def _ffn(x, g_arr, gidx, wgu, wd, layer, which, tm, final_g=None):
    m, d = x.shape
    f_dim = wd.shape[2]
    in_specs = [
        pl.BlockSpec((tm, d), lambda i: (i, 0)),
        pl.BlockSpec((None, 1, d), lambda i: (gidx, 0, 0)),
        _resident((None, None, d, f_dim), lambda i: (layer, which, 0, 0)),
        _resident((None, None, d, f_dim), lambda i: (layer, which, 0, 1)),
        _resident((None, None, f_dim, d), lambda i: (layer, which, 0, 0)),
    ]
    args = [x, g_arr, wgu, wgu, wd]
    if final_g is not None:
        in_specs.append(pl.BlockSpec((1, d), lambda i: (0, 0)))
        args.append(final_g)
    return pl.pallas_call(
        functools.partial(_ffn_kernel, final_norm=final_g is not None),
        grid=(m // tm,),
        in_specs=in_specs,
        out_specs=pl.BlockSpec((tm, d), lambda i: (i, 0)),
        out_shape=jax.ShapeDtypeStruct((m, d), F32),
        scratch_shapes=[pltpu.VMEM((tm, f_dim), BF16)],
        compiler_params=_cparams(("parallel",), FFN_VMEM_LIMIT),
        name="ffn",
    )(*args)


def _proj_kernel(*refs, cfg, n_out, has_rope, tn):
    x_ref, g_ref, w_ref = refs[:3]
    k = 3
    if has_rope:
        cos_ref, sin_ref = refs[3:5]
        k = 5
    outs = refs[k:k + n_out]
    h = _rms(x_ref[...], g_ref[...]).astype(BF16)
    if has_rope:
        cos, sin = cos_ref[...], sin_ref[...]
    for jj, (oi, rdim, scale, slot) in enumerate(cfg):
        acc = jnp.dot(h, w_ref[:, jj * tn:(jj + 1) * tn], preferred_element_type=F32)
        o = outs[oi]
        head_major = len(o.shape) == 3
        for c in range(tn // LANES):
            ch = acc[:, c * LANES:(c + 1) * LANES]
            if rdim:
                ch = _apply_rope(ch, cos, sin, rdim)
            if scale != 1.0:
                ch = ch * scale
            if head_major:
                o[slot + c] = ch.astype(o.dtype)
            else:
                o[:, slot * tn + c * LANES:slot * tn + (c + 1) * LANES] = ch.astype(o.dtype)


def _proj(x, g_arr, gidx, w, wlayer, wblock, cfg, out_defs, tm, tn, rope=None):
    m, kdim = x.shape
    nj = len(cfg)
    in_specs = [
        pl.BlockSpec((tm, kdim), lambda i: (i, 0)),
        pl.BlockSpec((None, 1, kdim), lambda i: (gidx, 0, 0)),
        _resident((None, kdim, nj * tn), lambda i: (wlayer, 0, wblock)),
    ]
    args = [x, g_arr, w]
    if rope is not None:
        cos, sin, nrb = rope
        in_specs += [pl.BlockSpec((tm, LANES), lambda i: (i % nrb, 0))] * 2
        args += [cos, sin]
    out_specs, out_shape = [], []
    for cols, dtype, head_major in out_defs:
        if head_major:
            out_specs.append(pl.BlockSpec((cols, tm, LANES), lambda i: (0, i, 0)))
            out_shape.append(jax.ShapeDtypeStruct((cols, m, LANES), dtype))
        else:
            out_specs.append(pl.BlockSpec((tm, cols), lambda i: (i, 0)))
            out_shape.append(jax.ShapeDtypeStruct((m, cols), dtype))
    return pl.pallas_call(
        functools.partial(_proj_kernel, cfg=tuple(cfg), n_out=len(out_defs), has_rope=rope is not None, tn=tn),
        grid=(m // tm,),
        in_specs=in_specs,
        out_specs=out_specs,
        out_shape=out_shape,
        compiler_params=_cparams(("parallel",)),
        name="proj",
    )(*args)


def _mem_kv(mem2d, g_arr, w, tm):
    depth, d, n = w.shape
    m = mem2d.shape[0]
    cfg = tuple((0, 0, 1.0, jj) for jj in range(n // d))
    (out,) = pl.pallas_call(
        functools.partial(_proj_kernel, cfg=cfg, n_out=1, has_rope=False, tn=d),
        grid=(depth, m // tm),
        in_specs=[
            pl.BlockSpec((tm, d), lambda l, i: (i, 0)),
            pl.BlockSpec((None, 1, d), lambda l, i: (l, 0, 0)),
            pl.BlockSpec((None, d, n), lambda l, i: (l, 0, 0)),
        ],
        out_specs=[pl.BlockSpec((None, tm, n), lambda l, i: (l, i, 0))],
        out_shape=[jax.ShapeDtypeStruct((depth, m, n), F32)],
        compiler_params=_cparams(("parallel", "parallel")),
        name="mem_kv",
    )(mem2d, g_arr, w)
    return out


def _mla_proj_kernel(x_ref, g_ref, win_ref, qg_ref, wq_ref, kvg_ref, wkv_ref, cos_ref, sin_ref,
                     q_ref, row_ref, k_ref, v_ref, *, q_lora, kv_lora, scale):
    h = _rms(x_ref[...], g_ref[...]).astype(BF16)
    p = jnp.dot(h, win_ref[...], preferred_element_type=F32)
    cq = p[:, :q_lora]
    ckv = p[:, q_lora:q_lora + kv_lora]
    kr2 = p[:, q_lora + kv_lora:]
    cos, sin = cos_ref[...], sin_ref[...]

    qn = _rms(cq, qg_ref[...]).astype(BF16)
    q = jnp.dot(qn, wq_ref[...], preferred_element_type=F32)
    for c in range(q.shape[1] // LANES):
        ch = q[:, c * LANES:(c + 1) * LANES]
        if c % 2 == 1:
            ch = _apply_rope(ch, cos, sin, MLA_ROPE)
        q_ref[:, c * LANES:(c + 1) * LANES] = (ch * scale).astype(q_ref.dtype)

    c_n = _rms(ckv, kvg_ref[...])
    kr_rot = _apply_rope(kr2, cos, sin, MLA_ROPE)
    row_ref[:, :kv_lora] = c_n
    row_ref[:, kv_lora:] = kr_rot[:, :MLA_ROPE]
    lane = lax.broadcasted_iota(jnp.int32, kr_rot.shape, 1)
    krp = jnp.where(lane < MLA_ROPE, kr_rot, 0.0).astype(k_ref.dtype)
    kv = jnp.dot(c_n.astype(BF16), wkv_ref[...], preferred_element_type=F32)
    n_nope = v_ref.shape[1] // MLA_V * MLA_NOPE
    for hh in range(n_nope // MLA_NOPE):
        k_ref[:, hh * MLA_SLOT:hh * MLA_SLOT + MLA_NOPE] = kv[:, hh * MLA_NOPE:(hh + 1) * MLA_NOPE].astype(k_ref.dtype)
        k_ref[:, hh * MLA_SLOT + MLA_NOPE:(hh + 1) * MLA_SLOT] = krp
    v_ref[...] = kv[:, n_nope:].astype(v_ref.dtype)


def _mla_proj(x, g_arr, gidx, win, qg, wq, kvg, wkv, layer, cos, sin, nrb, tm):
    m, d = x.shape
    q_lora, kv_lora = qg.shape[-1], kvg.shape[-1]
    n_in = win.shape[-1]
    nq = wq.shape[-1]
    nkv = wkv.shape[-1]
    nv = nkv * MLA_V // (MLA_NOPE + MLA_V)
    scale = (MLA_NOPE + MLA_ROPE) ** -0.5
    return pl.pallas_call(
        functools.partial(_mla_proj_kernel, q_lora=q_lora, kv_lora=kv_lora, scale=scale),
        grid=(m // tm,),
        in_specs=[
            pl.BlockSpec((tm, d), lambda i: (i, 0)),
            pl.BlockSpec((None, 1, d), lambda i: (gidx, 0, 0)),
            pl.BlockSpec((None, d, n_in), lambda i: (layer, 0, 0)),
            pl.BlockSpec((None, 1, q_lora), lambda i: (layer, 0, 0)),
            pl.BlockSpec((None, q_lora, nq), lambda i: (layer, 0, 0)),
            pl.BlockSpec((None, 1, kv_lora), lambda i: (layer, 0, 0)),
            pl.BlockSpec((None, kv_lora, nkv), lambda i: (layer, 0, 0)),
            pl.BlockSpec((tm, LANES), lambda i: (i % nrb, 0)),
            pl.BlockSpec((tm, LANES), lambda i: (i % nrb, 0)),
        ],
        out_specs=[
            pl.BlockSpec((tm, nq), lambda i: (i, 0)),
            pl.BlockSpec((tm, kv_lora + MLA_ROPE), lambda i: (i, 0)),
            pl.BlockSpec((tm, nq), lambda i: (i, 0)),
            pl.BlockSpec((tm, nv), lambda i: (i, 0)),
        ],
        out_shape=[
            jax.ShapeDtypeStruct((m, nq), BF16),
            jax.ShapeDtypeStruct((m, kv_lora + MLA_ROPE), F32),
            jax.ShapeDtypeStruct((m, nq), BF16),
            jax.ShapeDtypeStruct((m, nv), BF16),
        ],
        compiler_params=_cparams(("parallel",)),
        name="mla_proj",
    )(x, g_arr, win, qg, wq, kvg, wkv, cos, sin)


def _flash_update(s, v, m_ref, l_ref, acc_ref, idx, lo, hi):
    m_prev = m_ref[idx]
    m_new = jnp.maximum(m_prev, jnp.max(s, axis=-1, keepdims=True))
    alpha = jnp.exp(m_prev - m_new)
    p = jnp.exp(s - jnp.tile(m_new, (1, s.shape[1] // LANES)))
    l_ref[idx] = alpha * l_ref[idx] + jnp.sum(p, axis=-1, keepdims=True)
    acc_ref[:, lo:hi] = alpha * acc_ref[:, lo:hi] + jnp.dot(p.astype(BF16), v, preferred_element_type=F32)
    m_ref[idx] = m_new


def _causal_mask(tq, tk):
    row = lax.broadcasted_iota(jnp.int32, (tq, tk), 0)
    col = lax.broadcasted_iota(jnp.int32, (tq, tk), 1)
    return row >= col


def _causal_pairs(nq):
    pairs = [(qi, ki) for qi in range(nq) for ki in range(qi + 1)]
    qt = jnp.asarray([p[0] for p in pairs], jnp.int32)
    kt = jnp.asarray([p[1] for p in pairs], jnp.int32)
    return qt, kt


def _mla_attn_kernel(qt_ref, kt_ref, q_ref, k_ref, v_ref, o_ref, m_ref, l_ref, acc_ref, *, heads):
    p = pl.program_id(1)
    qi, ki = qt_ref[p], kt_ref[p]

    @pl.when(ki == 0)
    def _():
        m_ref[...] = jnp.full_like(m_ref, NEG_INF)
        l_ref[...] = jnp.zeros_like(l_ref)
        acc_ref[...] = jnp.zeros_like(acc_ref)

    def step(diagonal):
        mask = _causal_mask(q_ref.shape[0], k_ref.shape[0]) if diagonal else None
        for h in range(heads):
            s = _nt_dot(q_ref[:, h * MLA_SLOT:(h + 1) * MLA_SLOT], k_ref[:, h * MLA_SLOT:(h + 1) * MLA_SLOT])
            if diagonal:
                s = jnp.where(mask, s, NEG_INF)
            _flash_update(s, v_ref[:, h * MLA_V:(h + 1) * MLA_V], m_ref, l_ref, acc_ref,
                          h, h * MLA_V, (h + 1) * MLA_V)

    pl.when(ki < qi)(functools.partial(step, False))

    @pl.when(ki == qi)
    def _():
        step(True)
        for h in range(heads):
            lo, hi = h * MLA_V, (h + 1) * MLA_V
            o_ref[:, lo:hi] = (acc_ref[:, lo:hi] / l_ref[h]).astype(o_ref.dtype)


def _mla_attn(q, k, v, batch, seq, tq):
    m = q.shape[0]
    nq = seq // tq
    hv = v.shape[1]
    qt, kt = _causal_pairs(nq)
    qblk = lambda b, p, qt_, kt_: (b * nq + qt_[p], 0)
    kblk = lambda b, p, qt_, kt_: (b * nq + kt_[p], 0)
    return pl.pallas_call(
        functools.partial(_mla_attn_kernel, heads=MLA_HEADS),
        grid_spec=pltpu.PrefetchScalarGridSpec(
            num_scalar_prefetch=2,
            grid=(batch, qt.shape[0]),
            in_specs=[
                pl.BlockSpec((tq, q.shape[1]), qblk),
                pl.BlockSpec((tq, k.shape[1]), kblk),
                pl.BlockSpec((tq, hv), kblk),
            ],
            out_specs=pl.BlockSpec((tq, hv), qblk),
            scratch_shapes=[
                pltpu.VMEM((MLA_HEADS, tq, LANES), F32),
                pltpu.VMEM((MLA_HEADS, tq, LANES), F32),
                pltpu.VMEM((tq, hv), F32),
            ],
        ),
        out_shape=jax.ShapeDtypeStruct((m, hv), BF16),
        compiler_params=_cparams(("parallel", "arbitrary")),
        name="mla_attn",
    )(qt, kt, q, k, v)


def _diff_lambda(lam_ref, lam_init):
    lp = lam_ref[...]
    a = jnp.sum(lp[0:1] * lp[1:2], axis=-1, keepdims=True)
    b = jnp.sum(lp[2:3] * lp[3:4], axis=-1, keepdims=True)
    return jnp.exp(a) - jnp.exp(b) + lam_init


def _diff_attn_kernel(qt_ref, kt_ref, q_ref, k_ref, v_ref, lam_ref, o_ref, qc_ref, m_ref, l_ref,
                      acc0_ref, acc1_ref, *, heads, lam_init):
    p = pl.program_id(1)
    qi, ki = qt_ref[p], kt_ref[p]
    hd2 = LANES

    @pl.when(ki == 0)
    def _():
        m_ref[...] = jnp.full_like(m_ref, NEG_INF)
        l_ref[...] = jnp.zeros_like(l_ref)
        acc0_ref[...] = jnp.zeros_like(acc0_ref)
        acc1_ref[...] = jnp.zeros_like(acc1_ref)
        qf = q_ref[...].astype(F32)
        lane = lax.broadcasted_iota(jnp.int32, qf.shape, 1)
        first = (lane & (hd2 - 1)) < hd2 // 2
        qc_ref[0] = jnp.where(first, qf, 0.0).astype(BF16)
        qc_ref[1] = jnp.where(first, 0.0, qf).astype(BF16)

    def step(diagonal):
        mask = _causal_mask(q_ref.shape[0], k_ref.shape[0]) if diagonal else None
        for h in range(heads):
            lo, hi = h * hd2, (h + 1) * hd2
            kp = k_ref[:, lo:hi].astype(BF16)
            vh = v_ref[:, lo:hi].astype(BF16)
            for c, acc_ref in ((0, acc0_ref), (1, acc1_ref)):
                s = _nt_dot(qc_ref[c, :, lo:hi], kp)
                if diagonal:
                    s = jnp.where(mask, s, NEG_INF)
                _flash_update(s, vh, m_ref, l_ref, acc_ref, 2 * h + c, lo, hi)

    pl.when(ki < qi)(functools.partial(step, False))

    @pl.when(ki == qi)
    def _():
        step(True)
        lam = _diff_lambda(lam_ref, lam_init)
        for h in range(heads):
            lo, hi = h * hd2, (h + 1) * hd2
            o_ref[:, lo:hi] = (acc0_ref[:, lo:hi] / l_ref[2 * h]
                               - lam * (acc1_ref[:, lo:hi] / l_ref[2 * h + 1]))


def _diff_attn(q, k, v, lam_arr, layer, lam_init, batch, seq, tq):
    m, w = q.shape
    nq = seq // tq
    qt, kt = _causal_pairs(nq)
    qblk = lambda b, p, qt_, kt_: (b * nq + qt_[p], 0)
    kblk = lambda b, p, qt_, kt_: (b * nq + kt_[p], 0)
    return pl.pallas_call(
        functools.partial(_diff_attn_kernel, heads=DIFF_HEADS, lam_init=lam_init),
        grid_spec=pltpu.PrefetchScalarGridSpec(
            num_scalar_prefetch=2,
            grid=(batch, qt.shape[0]),
            in_specs=[
                pl.BlockSpec((tq, w), qblk),
                pl.BlockSpec((tq, w), kblk),
                pl.BlockSpec((tq, w), kblk),
                pl.BlockSpec((None,) + lam_arr.shape[1:], lambda b, p, qt_, kt_: (layer, 0, 0)),
            ],
            out_specs=pl.BlockSpec((tq, w), qblk),
            scratch_shapes=[
                pltpu.VMEM((2, tq, w), BF16),
                pltpu.VMEM((2 * DIFF_HEADS, tq, LANES), F32),
                pltpu.VMEM((2 * DIFF_HEADS, tq, LANES), F32),
                pltpu.VMEM((tq, w), F32),
                pltpu.VMEM((tq, w), F32),
            ],
        ),
        out_shape=jax.ShapeDtypeStruct((m, w), F32),
        compiler_params=_cparams(("parallel", "arbitrary")),
        name="diff_attn",
    )(qt, kt, q, k, v, lam_arr)


def _dil_attn_kernel(*refs, heads, d, nbs, span, has_prev):
    if has_prev:
        q_ref, kc_ref, vc_ref, kp_ref, vp_ref, o_ref, lse_ref = refs
    else:
        q_ref, kc_ref, vc_ref, o_ref, lse_ref = refs
    first = pl.program_id(1) == 0
    nq = DIL_BLOCK
    nkeys = 2 * nq if has_prev else nq
    qi = lax.broadcasted_iota(jnp.int32, (nq, nkeys), 0)
    kj = lax.broadcasted_iota(jnp.int32, (nq, nkeys), 1)
    dist = qi + (nkeys - nq) - kj
    band = (dist >= 0) & (dist <= span)
    rows = lambda r, i: pl.ds(r + d * nq * i, nq, stride=d) if d > 1 else pl.ds(nq * i, nq)
    for r in range(d):
        for i in range(nbs):
            cur = rows(r, i)
            mask = band
            if has_prev and i == 0:
                mask = band & ((kj >= nq) | jnp.logical_not(first))
            for h in range(heads):
                kh, vh = kc_ref[h, cur, :], vc_ref[h, cur, :]
                if has_prev:
                    if i == 0:
                        kprev, vprev = kp_ref[h, rows(r, 0), :], vp_ref[h, rows(r, 0), :]
                    else:
                        kprev, vprev = kc_ref[h, rows(r, i - 1), :], vc_ref[h, rows(r, i - 1), :]
                    kh = jnp.concatenate([kprev, kh], axis=0)
                    vh = jnp.concatenate([vprev, vh], axis=0)
                s = jnp.where(mask, _nt_dot(q_ref[h, cur, :].astype(BF16), kh.astype(BF16)), NEG_INF)
                m = jnp.max(s, axis=-1, keepdims=True)
                p = jnp.where(mask, jnp.exp(s - m), 0.0)
                l = jnp.sum(p, axis=-1, keepdims=True)
                o = jnp.dot(p.astype(BF16), vh.astype(BF16), preferred_element_type=F32)
                o_ref[h, cur, :] = o / l
                lse_ref[h, cur, :] = jnp.broadcast_to(m + jnp.log(l), (nq, LANES))


def _dil_attn(q, kv, batch, seq, w, d):
    heads, m, hd = q.shape
    stride_rows = DIL_BLOCK * d
    n_rb = seq // stride_rows
    has_prev = n_rb > 1
    nbs = min(n_rb, max(1, 512 // stride_rows))
    tb = stride_rows * nbs
    steps = seq // tb
    hc = max(1, min(heads, 32 // (d * nbs)))
    n_hc = heads // hc
    cur = lambda b, n, c: (c, b * steps + n, 0)
    cur_v = lambda b, n, c: (n_hc + c, b * steps + n, 0)
    blk = (hc, tb, hd)
    in_specs = [pl.BlockSpec(blk, cur), pl.BlockSpec(blk, cur), pl.BlockSpec(blk, cur_v)]
    args = [q, kv, kv]
    if has_prev:
        prev = lambda b, n: b * n_rb + jnp.maximum(n * nbs - 1, 0)
        in_specs += [pl.BlockSpec((hc, stride_rows, hd), lambda b, n, c: (c, prev(b, n), 0)),
                     pl.BlockSpec((hc, stride_rows, hd), lambda b, n, c: (n_hc + c, prev(b, n), 0))]
        args += [kv, kv]
    return pl.pallas_call(
        functools.partial(_dil_attn_kernel, heads=hc, d=d, nbs=nbs, span=w // d, has_prev=has_prev),
        grid=(batch, steps, n_hc),
        in_specs=in_specs,
        out_specs=[pl.BlockSpec(blk, cur)] * 2,
        out_shape=[jax.ShapeDtypeStruct(q.shape, F32)] * 2,
        compiler_params=_cparams(("parallel", "parallel", "parallel")),
        name="dil_attn",
    )(*args)


def _combine_groups(os_, lses):
    m = functools.reduce(jnp.maximum, lses)
    es = [jnp.exp(l - m) for l in lses]
    den = functools.reduce(lambda a, b: a + b, es)
    num = functools.reduce(lambda a, b: a + b, [e * o for e, o in zip(es, os_)])
    return num / den


def _out_kernel(*refs, mode, n_groups, heads, gain_scale):
    x_ref, w_ref = refs[:2]
    o_ref = refs[-1]
    ins = refs[2:-1]
    if mode == "plain":
        a = ins[0][...]
    elif mode == "dil":
        parts = []
        for h in range(ins[0].shape[0]):
            parts.append(_combine_groups([r[h] for r in ins[:n_groups]], [r[h] for r in ins[n_groups:]]))
        a = jnp.concatenate(parts, axis=-1)
    else:
        o, g = ins[0], ins[1][...]
        parts = []
        for h in range(heads):
            parts.append(_rms(o[:, h * LANES:(h + 1) * LANES], g) * gain_scale)
        a = jnp.concatenate(parts, axis=-1)
    o_ref[...] = x_ref[...] + jnp.dot(a.astype(BF16), w_ref[...], preferred_element_type=F32)


def _out_proj(x, w, layer, ins, tm, mode="plain", gain=None, gain_scale=1.0):
    m, d = x.shape
    kdim = w.shape[1]
    in_specs = [pl.BlockSpec((tm, d), lambda i: (i, 0)),
                pl.BlockSpec((None, kdim, d), lambda i: (layer, 0, 0))]
    args = [x, w]
    for a in ins:
        if a.ndim == 3:
            in_specs.append(pl.BlockSpec((a.shape[0], tm, a.shape[2]), lambda i: (0, i, 0)))
        else:
            in_specs.append(pl.BlockSpec((tm, a.shape[1]), lambda i: (i, 0)))
        args.append(a)
    if gain is not None:
        garr, gl = gain
        in_specs.append(pl.BlockSpec((None, 1, garr.shape[-1]), lambda i: (gl, 0, 0)))
        args.append(garr)
    return pl.pallas_call(
        functools.partial(_out_kernel, mode=mode, n_groups=len(DIL_GROUPS), heads=DIFF_HEADS,
                          gain_scale=gain_scale),
        grid=(m // tm,),
        in_specs=in_specs,
        out_specs=pl.BlockSpec((tm, d), lambda i: (i, 0)),
        out_shape=jax.ShapeDtypeStruct((m, d), F32),
        compiler_params=_cparams(("parallel",)),
        name="out_proj",
    )(*args)


def _cross_kernel(x_ref, g_ref, wq_ref, mkv_ref, wo_ref, o_ref, *, heads, scale):
    x = x_ref[...]
    h = _rms(x, g_ref[...]).astype(BF16)
    q = (jnp.dot(h, wq_ref[...], preferred_element_type=F32) * scale).astype(BF16)
    hd = q.shape[1] // heads
    hw = heads * hd
    parts = []
    for hh in range(heads):
        k = mkv_ref[:, hh * hd:(hh + 1) * hd].astype(BF16)
        v = mkv_ref[:, hw + hh * hd:hw + (hh + 1) * hd].astype(BF16)
        s = _nt_dot(q[:, hh * hd:(hh + 1) * hd], k)
        m = jnp.max(s, axis=-1, keepdims=True)
        p = jnp.exp(s - m)
        p = p / jnp.sum(p, axis=-1, keepdims=True)
        parts.append(jnp.dot(p.astype(BF16), v, preferred_element_type=F32).astype(BF16))
    o = jnp.concatenate(parts, axis=-1)
    o_ref[...] = x + jnp.dot(o, wo_ref[...], preferred_element_type=F32)


def _cross_prompt(x, g_arr, gidx, wq, wo, layer, mkv, seq, tm):
    m, d = x.shape
    nrb = seq // tm
    mem_len, kvw = mkv.shape[2], mkv.shape[3]
    hd = d // X_HEADS
    return pl.pallas_call(
        functools.partial(_cross_kernel, heads=X_HEADS, scale=hd ** -0.5),
        grid=(m // tm,),
        in_specs=[
            pl.BlockSpec((tm, d), lambda i: (i, 0)),
            pl.BlockSpec((None, 1, d), lambda i: (gidx, 0, 0)),
            pl.BlockSpec((None, d, d), lambda i: (layer, 0, 0)),
            pl.BlockSpec((None, None, mem_len, kvw), lambda i: (layer, i // nrb, 0, 0)),
            pl.BlockSpec((None, d, d), lambda i: (layer, 0, 0)),
        ],
        out_specs=pl.BlockSpec((tm, d), lambda i: (i, 0)),
        out_shape=jax.ShapeDtypeStruct((m, d), F32),
        compiler_params=_cparams(("parallel",)),
        name="cross_prompt",
    )(x, g_arr, wq, mkv, wo)


def _block_diag(qrow, rows, head_w):
    w = qrow.shape[1]
    r = lax.broadcasted_iota(jnp.int32, (rows, w), 0)
    lane = lax.broadcasted_iota(jnp.int32, (rows, w), 1)
    sel = (lane >= r * head_w) & (lane < (r + 1) * head_w)
    qb = jnp.where(sel, jnp.broadcast_to(qrow.astype(F32), (rows, w)), 0.0)
    return qb.astype(BF16), sel


def _chunk_rows(x, heads, chunks):
    lead = x.shape[:-1]
    x = x.reshape(*lead, heads, chunks, LANES)
    return jnp.swapaxes(x, -3, -2).reshape(*lead, chunks * heads, LANES)


def _unchunk_rows(x, heads, chunks):
    lead = x.shape[:-2]
    x = x.reshape(*lead, chunks, heads, LANES)
    return jnp.swapaxes(x, -3, -2).reshape(*lead, heads * chunks * LANES)


def _cross_sample_kernel(q_ref, mkv_ref, o_ref, *, heads):
    q = _bf16_round(q_ref[...])
    k = _bf16_round(mkv_ref[:, 0])
    v = _bf16_round(mkv_ref[:, 1])
    part = jnp.sum(k * q[None], axis=-1, keepdims=True)
    rows = part.shape[1]
    s = part
    for j in range(1, rows // heads):
        s = s + jnp.concatenate([part[:, j * heads:], part[:, :j * heads]], axis=1)
    m = jnp.max(s, axis=0)
    p = jnp.exp(s - m[None])
    l = jnp.sum(p, axis=0)
    o_ref[...] = jnp.sum(_bf16_round(p) * v, axis=0) / l


def _cross_sample(q, mem_kv, layer):
    db, rows, _ = q.shape
    mem_len = mem_kv.shape[2]
    return pl.pallas_call(
        functools.partial(_cross_sample_kernel, heads=X_HEADS),
        grid=(db,),
        in_specs=[
            pl.BlockSpec((None, rows, LANES), lambda b: (b, 0, 0)),
            pl.BlockSpec((None, None, mem_len, 2, rows, LANES), lambda b: (layer, b, 0, 0, 0, 0)),
        ],
        out_specs=pl.BlockSpec((None, rows, LANES), lambda b: (b, 0, 0)),
        out_shape=jax.ShapeDtypeStruct((db, rows, LANES), F32),
        compiler_params=_cparams(("parallel",)),
        name="cross_sample",
    )(q, mem_kv)


def _bf16_round(x):
    return x.astype(BF16).astype(F32)


def _dil_sample_kernel(*refs, n_groups):
    q_ref, kv_ref = refs[:2]
    bufs = refs[2:2 + n_groups]
    o_ref = refs[2 + n_groups]
    outs, lses = [], []
    for g in range(n_groups):
        q = _bf16_round(q_ref[g])
        k_new = _bf16_round(kv_ref[2 * g])
        v_new = _bf16_round(kv_ref[2 * g + 1])
        k = _bf16_round(bufs[g][:, 0])
        v = _bf16_round(bufs[g][:, 1])
        s = jnp.sum(k * q[None], axis=-1, keepdims=True)
        s_new = jnp.sum(q * k_new, axis=-1, keepdims=True)
        m = jnp.maximum(jnp.max(s, axis=0), s_new)
        p = jnp.exp(s - m[None])
        p_new = jnp.exp(s_new - m)
        l = jnp.sum(p, axis=0) + p_new
        o = jnp.sum(_bf16_round(p) * v, axis=0) + _bf16_round(p_new) * v_new
        outs.append(o / l)
        lses.append(m + jnp.log(l))
    o_ref[...] = _combine_groups(outs, lses)


def _dil_sample(q, kv_new, bufs, layer):
    db, n_groups, heads, hd = q.shape
    in_specs = [
        pl.BlockSpec((None, n_groups, heads, hd), lambda b: (b, 0, 0, 0)),
        pl.BlockSpec((None, 2 * n_groups, heads, hd), lambda b: (b, 0, 0, 0)),
    ]
    args = [q, kv_new]
    for (wg, d), buf in zip(DIL_GROUPS, bufs):
        wb = buf.shape[2]
        assert wb == wg and wb % d == 0, "window buffer must hold exactly the group's window"
        args.append(buf.reshape(buf.shape[0], db, wb // d, d, 2, heads, hd))
        in_specs.append(pl.BlockSpec((None, None, wb // d, None, 2, heads, hd),
                                     lambda b: (layer, b, 0, 0, 0, 0, 0)))
    return pl.pallas_call(
        functools.partial(_dil_sample_kernel, n_groups=n_groups),
        grid=(db,),
        in_specs=in_specs,
        out_specs=pl.BlockSpec((None, heads, hd), lambda b: (b, 0, 0)),
        out_shape=jax.ShapeDtypeStruct((db, heads, hd), F32),
        compiler_params=_cparams(("parallel",)),
        name="dil_sample",
    )(*args)


def _mla_absorb_q_kernel(q_ref, wkv_ref, o_ref, *, heads, kv_lora):
    slot_out = kv_lora + LANES
    for h in range(heads):
        qn = q_ref[:, h * MLA_SLOT:h * MLA_SLOT + MLA_NOPE]
        w_uk = wkv_ref[:, h * MLA_NOPE:(h + 1) * MLA_NOPE]
        o_ref[:, h * slot_out:h * slot_out + kv_lora] = _nt_dot(qn, w_uk).astype(o_ref.dtype)
        o_ref[:, h * slot_out + kv_lora:(h + 1) * slot_out] = (
            q_ref[:, h * MLA_SLOT + MLA_NOPE:(h + 1) * MLA_SLOT].astype(o_ref.dtype))


def _mla_absorb_q(q, wkv, layer):
    db = q.shape[0]
    kv_lora, nkv = wkv.shape[1], wkv.shape[2]
    slot_out = kv_lora + LANES
    return pl.pallas_call(
        functools.partial(_mla_absorb_q_kernel, heads=MLA_HEADS, kv_lora=kv_lora),
        grid=(1,),
        in_specs=[pl.BlockSpec(q.shape, lambda i: (0, 0)),
                  pl.BlockSpec((None, kv_lora, nkv), lambda i: (layer, 0, 0))],
        out_specs=pl.BlockSpec((db, MLA_HEADS * slot_out), lambda i: (0, 0)),
        out_shape=jax.ShapeDtypeStruct((db, MLA_HEADS * slot_out), F32),
        compiler_params=_cparams(("arbitrary",)),
        name="mla_absorb_q",
    )(q, wkv)


def _mla_paged_kernel(*refs, pages, kv_lora, row_w):
    pt_ref, q_ref, new_ref = refs[:3]
    page_refs = refs[3:3 + pages]
    o_ref = refs[3 + pages]
    rows_ref, m_ref, l_ref, acc_ref = refs[4 + pages:]
    del pt_ref
    g = pl.program_id(1)
    kpad = rows_ref.shape[0]

    @pl.when(g == 0)
    def _():
        rows_ref[row_w:, :] = jnp.zeros((kpad - row_w, rows_ref.shape[1]), BF16)
        new = _bf16_round(new_ref[...])
        qf = _bf16_round(q_ref[:, :row_w])
        m_ref[...] = jnp.sum(qf * new, axis=-1, keepdims=True)
        l_ref[...] = jnp.ones_like(l_ref)
        acc_ref[...] = jnp.broadcast_to(new[:, :kv_lora], acc_ref.shape)

    for k in range(pages):
        rows_ref[:row_w, k * PAGE_SIZE:(k + 1) * PAGE_SIZE] = page_refs[k][...].astype(BF16)
    s = jnp.dot(q_ref[...].astype(BF16), rows_ref[...], preferred_element_type=F32)
    m_prev = m_ref[...]
    m_new = jnp.maximum(m_prev, jnp.max(s, axis=-1, keepdims=True))
    alpha = jnp.exp(m_prev - m_new)
    p = jnp.exp(s - m_new)
    l_ref[...] = alpha * l_ref[...] + jnp.sum(p, axis=-1, keepdims=True)
    acc_ref[...] = alpha * acc_ref[...] + _nt_dot(p.astype(BF16), rows_ref[:kv_lora, :])
    m_ref[...] = m_new

    @pl.when(g == pl.num_programs(1) - 1)
    def _():
        o_ref[...] = (acc_ref[...] / l_ref[...]).astype(o_ref.dtype)


def _mla_paged(q_abs, row_new, cache, layer, page_table, pages):
    db, heads, kpad = q_abs.shape
    row_w = cache.shape[2]
    kv_lora = row_w - MLA_ROPE
    n_pages = page_table.shape[1]
    assert n_pages % pages == 0
    in_specs = [
        pl.BlockSpec((None, heads, kpad), lambda b, g, pt: (b, 0, 0)),
        pl.BlockSpec((None, 1, row_w), lambda b, g, pt: (b, 0, 0)),
    ]
    for k in range(pages):
        in_specs.append(pl.BlockSpec((None, None, row_w, PAGE_SIZE),
                                     lambda b, g, pt, k=k: (layer, pt[b, g * pages + k], 0, 0)))
    return pl.pallas_call(
        functools.partial(_mla_paged_kernel, pages=pages, kv_lora=kv_lora, row_w=row_w),
        grid_spec=pltpu.PrefetchScalarGridSpec(
            num_scalar_prefetch=1,
            grid=(db, n_pages // pages),
            in_specs=in_specs,
            out_specs=pl.BlockSpec((None, heads, kv_lora), lambda b, g, pt: (b, 0, 0)),
            scratch_shapes=[
                pltpu.VMEM((kpad, pages * PAGE_SIZE), BF16),
                pltpu.VMEM((heads, 1), F32),
                pltpu.VMEM((heads, 1), F32),
                pltpu.VMEM((heads, kv_lora), F32),
            ],
        ),
        out_shape=jax.ShapeDtypeStruct((db, heads, kv_lora), F32),
        compiler_params=_cparams(("parallel", "arbitrary")),
        name="mla_paged",
    )(page_table, q_abs, row_new, *([cache] * pages))


def _mla_absorb_o_kernel(o_ref, wkv_ref, out_ref, *, heads, kv_lora):
    v0 = heads * MLA_NOPE
    for h in range(heads):
        w_uv = wkv_ref[:, v0 + h * MLA_V:v0 + (h + 1) * MLA_V]
        out_ref[:, h * MLA_V:(h + 1) * MLA_V] = jnp.dot(
            o_ref[:, h * kv_lora:(h + 1) * kv_lora].astype(BF16), w_uv,
            preferred_element_type=F32).astype(out_ref.dtype)


def _mla_absorb_o(o_lat, wkv, layer):
    db = o_lat.shape[0]
    kv_lora, nkv = wkv.shape[1], wkv.shape[2]
    return pl.pallas_call(
        functools.partial(_mla_absorb_o_kernel, heads=MLA_HEADS, kv_lora=kv_lora),
        grid=(1,),
        in_specs=[pl.BlockSpec(o_lat.shape, lambda i: (0, 0)),
                  pl.BlockSpec((None, kv_lora, nkv), lambda i: (layer, 0, 0))],
        out_specs=pl.BlockSpec((db, MLA_HEADS * MLA_V), lambda i: (0, 0)),
        out_shape=jax.ShapeDtypeStruct((db, MLA_HEADS * MLA_V), BF16),
        compiler_params=_cparams(("arbitrary",)),
        name="mla_absorb_o",
    )(o_lat, wkv)


def _diff_paged_kernel(*refs, pages, heads, lam_init):
    pt_ref, q_ref, kn_ref, vn_ref, lam_ref = refs[:5]
    k_refs = refs[5:5 + pages]
    v_refs = refs[5 + pages:5 + 2 * pages]
    o_ref = refs[5 + 2 * pages]
    qb_ref, m_ref, l_ref, acc_ref = refs[6 + 2 * pages:]
    del pt_ref
    g = pl.program_id(1)
    comps = 2 * heads
    half = LANES // 2

    @pl.when(g == 0)
    def _():
        qb, _ = _block_diag(q_ref[...].astype(BF16), comps, half)
        qb_ref[...] = qb
        kn = kn_ref[...].astype(BF16).astype(F32)
        m_ref[...] = jnp.sum(qb.astype(F32) * kn, axis=-1, keepdims=True)
        l_ref[...] = jnp.ones_like(l_ref)
        acc_ref[...] = jnp.broadcast_to(vn_ref[...].astype(BF16).astype(F32), acc_ref.shape)

    qb = qb_ref[...]
    s = jnp.concatenate([jnp.dot(qb, k_refs[k][...].astype(BF16), preferred_element_type=F32)
                         for k in range(pages)], axis=-1)
    m_prev = m_ref[...]
    m_new = jnp.maximum(m_prev, jnp.max(s, axis=-1, keepdims=True))
    alpha = jnp.exp(m_prev - m_new)
    p32 = jnp.exp(s - m_new)
    l_ref[...] = alpha * l_ref[...] + jnp.sum(p32, axis=-1, keepdims=True)
    p = p32.astype(BF16)
    for h in range(heads):
        vh = jnp.concatenate([v_refs[k][:, h, :] for k in range(pages)], axis=0).astype(BF16)
        lo, hi = h * LANES, (h + 1) * LANES
        acc_ref[:, lo:hi] = alpha * acc_ref[:, lo:hi] + jnp.dot(p, vh, preferred_element_type=F32)
    m_ref[...] = m_new

    @pl.when(g == pl.num_programs(1) - 1)
    def _():
        lam = _diff_lambda(lam_ref, lam_init)
        of = acc_ref[...] / l_ref[...]
        r = lax.broadcasted_iota(jnp.int32, of.shape, 0)
        head = lax.shift_right_logical(lax.broadcasted_iota(jnp.int32, of.shape, 1), 7)
        pos = jnp.sum(jnp.where(r == 2 * head, of, 0.0), axis=0, keepdims=True)
        neg = jnp.sum(jnp.where(r == 2 * head + 1, of, 0.0), axis=0, keepdims=True)
        o_ref[...] = pos - lam * neg


def _diff_paged(q, k_new, v_new, lam_arr, layer, lam_init, pool_k, pool_v, page_table, pages):
    db, _, w = q.shape
    n_pages = page_table.shape[1]
    assert n_pages % pages == 0
    row = lambda b, g, pt: (b, 0, 0)
    in_specs = [
        pl.BlockSpec((None, 1, w), row),
        pl.BlockSpec((None, 1, w), row),
        pl.BlockSpec((None, 1, w), row),
        pl.BlockSpec((None,) + lam_arr.shape[1:], lambda b, g, pt: (layer, 0, 0)),
    ]
    for k in range(pages):
        in_specs.append(pl.BlockSpec((None, None, w, PAGE_SIZE),
                                     lambda b, g, pt, k=k: (layer, pt[b, g * pages + k], 0, 0)))
    for k in range(pages):
        in_specs.append(pl.BlockSpec((None, None, PAGE_SIZE, DIFF_HEADS, LANES),
                                     lambda b, g, pt, k=k: (layer, pt[b, g * pages + k], 0, 0, 0)))
    comps = 2 * DIFF_HEADS
    return pl.pallas_call(
        functools.partial(_diff_paged_kernel, pages=pages, heads=DIFF_HEADS, lam_init=lam_init),
        grid_spec=pltpu.PrefetchScalarGridSpec(
            num_scalar_prefetch=1,
            grid=(db, n_pages // pages),
            in_specs=in_specs,
            out_specs=pl.BlockSpec((None, 1, w), row),
            scratch_shapes=[
                pltpu.VMEM((comps, w), BF16),
                pltpu.VMEM((comps, 1), F32),
                pltpu.VMEM((comps, 1), F32),
                pltpu.VMEM((comps, w), F32),
            ],
        ),
        out_shape=jax.ShapeDtypeStruct((db, 1, w), F32),
        compiler_params=_cparams(("parallel", "arbitrary")),
        name="diff_paged",
    )(page_table, q, k_new, v_new, lam_arr, *([pool_k] * pages), *([pool_v] * pages))


def _mla_weights(w_in, w_q_up, w_kv_up):
    n, q_lora = w_q_up.shape[0], w_q_up.shape[1]
    kv_lora = w_kv_up.shape[1]
    win = jnp.concatenate([w_in, w_in[:, :, -MLA_ROPE:]], axis=-1).astype(BF16)
    wq = w_q_up.reshape(n, q_lora, MLA_HEADS, MLA_NOPE + MLA_ROPE)
    pad = jnp.zeros((n, q_lora, MLA_HEADS, MLA_SLOT - MLA_NOPE - MLA_ROPE), w_q_up.dtype)
    wq = jnp.concatenate([wq, pad], axis=-1).reshape(n, q_lora, MLA_HEADS * MLA_SLOT).astype(BF16)
    wkv = w_kv_up.reshape(n, kv_lora, MLA_HEADS, MLA_NOPE + MLA_V)
    wkv = jnp.concatenate([wkv[..., :MLA_NOPE].reshape(n, kv_lora, -1),
                           wkv[..., MLA_NOPE:].reshape(n, kv_lora, -1)], axis=-1).astype(BF16)
    return win, wq, wkv


def kernel(x_prompt, x_sample, cache_mla, cache_dil_g0, cache_dil_g1, cache_dil_g2, cache_diff_k, cache_diff_v, cache_mem_kv, page_table, mem_prompt, norm_g, mem_norm_g, final_norm_g, ffn_w_gu, ffn_w_down, x_wq, x_wkv, x_wo, mla_w_in, mla_q_norm_g, mla_w_q_up, mla_kv_norm_g, mla_w_kv_up, mla_w_out, dil_w_in, dil_w_out, diff_w_in, diff_lambda, diff_subln_g, diff_w_out):
    bp, sp, d = x_prompt.shape
    bs, ts, _ = x_sample.shape
    assert ts == 1, "sample path handles one new token per sequence"
    depth = norm_g.shape[0]
    past_len = page_table.shape[1] * PAGE_SIZE
    n_pg = sp // PAGE_SIZE
    mem_len = mem_prompt.shape[1]
    n_groups = len(DIL_GROUPS)
    dil_bufs = (cache_dil_g0, cache_dil_g1, cache_dil_g2)
    hw = DIL_HEADS * LANES

    tm_p = 512
    tm_ffn = 1024
    tm_s = bs
    tq = 512
    nrb_p = sp // tm_p

    wgu, wdn = ffn_w_gu.astype(BF16), ffn_w_down.astype(BF16)
    wq_x, wkv_x, wo_x = x_wq.astype(BF16), x_wkv.astype(BF16), x_wo.astype(BF16)
    mla_win, mla_wq, mla_wkv = _mla_weights(mla_w_in, mla_w_q_up, mla_w_kv_up)
    mla_wo = mla_w_out.astype(BF16)
    dil_win, dil_wo = dil_w_in.astype(BF16), dil_w_out.astype(BF16)
    diff_win, diff_wo = diff_w_in.astype(BF16), diff_w_out.astype(BF16)

    g_arr = norm_g.reshape(depth * 4, 1, d)
    mem_g = mem_norm_g.reshape(depth, 1, d)
    mla_qg = mla_q_norm_g.reshape(mla_q_norm_g.shape[0], 1, -1)
    mla_kvg = mla_kv_norm_g.reshape(mla_kv_norm_g.shape[0], 1, -1)
    subln_g = diff_subln_g.reshape(diff_subln_g.shape[0], 1, -1)

    pos_p = jnp.arange(sp, dtype=jnp.int32)
    pos_s = jnp.full((bs,), past_len, dtype=jnp.int32)
    rope_p = {dim: _rope_tables(pos_p, dim) for dim in (64, 128)}
    rope_s = {dim: _rope_tables(pos_s, dim) for dim in (64, 128)}

    x_chunks = d // X_HEADS // LANES
    cache_mem = _chunk_rows(cache_mem_kv.reshape(depth, bs, mem_len, 2, d), X_HEADS, x_chunks)
    mem2d = mem_prompt.reshape(bp * mem_len, d)
    mkv_all = _mem_kv(mem2d, mem_g, wkv_x, min(512, mem2d.shape[0])).reshape(depth, bp, mem_len, -1)
    pool_k = jnp.transpose(cache_diff_k, (0, 1, 3, 4, 2)).reshape(
        cache_diff_k.shape[0], cache_diff_k.shape[1], -1, PAGE_SIZE)
    pool_v = cache_diff_v
    cache_mla_t = jnp.swapaxes(cache_mla, 2, 3)

    xp = x_prompt.reshape(bp * sp, d)
    xs = x_sample.reshape(bs, d)

    mla_p, mla_s, diff_kp, diff_ks, diff_vp, diff_vs = [], [], [], [], [], []
    dil_p = [[] for _ in DIL_GROUPS]
    dil_s = [[] for _ in DIL_GROUPS]

    for i in range(depth):
        kind, j = i % N_MIXERS, i // N_MIXERS
        xp = _ffn(xp, g_arr, 4 * i, wgu, wdn, i, 0, tm_ffn)
        xs = _ffn(xs, g_arr, 4 * i, wgu, wdn, i, 0, tm_s)

        if kind == 0:
            cos, sin = rope_p[MLA_ROPE]
            q, row, k, v = _mla_proj(xp, g_arr, 4 * i + 1, mla_win, mla_qg, mla_wq, mla_kvg, mla_wkv,
                                     j, cos, sin, nrb_p, tm_p)
            o_p = _mla_attn(q, k, v, bp, sp, tq)
            mla_p.append(row.reshape(bp, n_pg, PAGE_SIZE, -1))
            xp = _out_proj(xp, mla_wo, j, [o_p], tm_p)

            cos, sin = rope_s[MLA_ROPE]
            q, row, _, _ = _mla_proj(xs, g_arr, 4 * i + 1, mla_win, mla_qg, mla_wq, mla_kvg, mla_wkv,
                                     j, cos, sin, 1, tm_s)
            q_abs = _mla_absorb_q(q, mla_wkv, j).reshape(bs, MLA_HEADS, -1)
            o_lat = _mla_paged(q_abs, row.reshape(bs, 1, -1), cache_mla_t, j, page_table, MLA_PAGES_PER_STEP)
            o_s = _mla_absorb_o(o_lat.reshape(bs, -1), mla_wkv, j)
            mla_s.append(row.reshape(bs, ts, -1))
            xs = _out_proj(xs, mla_wo, j, [o_s], tm_s)
        elif kind == 1:
            scale = LANES ** -0.5
            os_p, lses_p, qs_s, kvs_s = [], [], [], []
            for g, (w, dd) in enumerate(DIL_GROUPS):
                cfg = [(0, LANES, scale, 0), (1, LANES, 1.0, 0), (1, 0, 1.0, DIL_HEADS)]
                outs = [(DIL_HEADS, F32, True), (2 * DIL_HEADS, F32, True)]
                q, kv = _proj(xp, g_arr, 4 * i + 1, dil_win, j, g, cfg, outs, tm_p, hw,
                              rope=(*rope_p[LANES], nrb_p))
                o, lse = _dil_attn(q, kv, bp, sp, w, dd)
                os_p.append(o)
                lses_p.append(lse)
                wl = min(w, sp)
                kv_tail = kv.reshape(2, DIL_HEADS, bp, sp, LANES)[:, :, :, sp - wl:]
                dil_p[g].append(jnp.transpose(kv_tail, (2, 3, 0, 1, 4)))
                q, kv = _proj(xs, g_arr, 4 * i + 1, dil_win, j, g, cfg, outs, tm_s, hw,
                              rope=(*rope_s[LANES], 1))
                qs_s.append(jnp.swapaxes(q, 0, 1))
                kv = jnp.swapaxes(kv, 0, 1)
                kvs_s.append(kv)
                dil_s[g].append(kv.reshape(bs, ts, 2, DIL_HEADS, LANES))
            xp = _out_proj(xp, dil_wo, j, os_p + lses_p, tm_p, mode="dil")
            q_all = jnp.stack(qs_s, axis=1)
            kv_all = jnp.concatenate(kvs_s, axis=1).reshape(bs, 2 * n_groups, DIL_HEADS, LANES)
            o_s = _dil_sample(q_all, kv_all, dil_bufs, j)
            xs = _out_proj(xs, dil_wo, j, [o_s.reshape(bs, -1)], tm_s)
        else:
            lam_init = 0.8 - 0.6 * math.exp(-0.3 * i)
            dw = diff_win.shape[-1] // 3
            scale = (LANES // 2) ** -0.5
            cfg = [(0, LANES // 2, scale, 0), (1, LANES // 2, 1.0, 0), (2, 0, 1.0, 0)]
            outs = [(dw, BF16, False), (dw, F32, False), (dw, F32, False)]
            q, k, v = _proj(xp, g_arr, 4 * i + 1, diff_win, j, 0, cfg, outs, tm_p, dw,
                            rope=(*rope_p[LANES // 2], nrb_p))
            o_p = _diff_attn(q, k, v, diff_lambda, j, lam_init, bp, sp, tq)
            diff_kp.append(k.reshape(bp, n_pg, PAGE_SIZE, 2 * DIFF_HEADS, LANES // 2))
            diff_vp.append(v.reshape(bp, n_pg, PAGE_SIZE, DIFF_HEADS, LANES))
            xp = _out_proj(xp, diff_wo, j, [o_p], tm_p, mode="diff", gain=(subln_g, j),
                           gain_scale=1.0 - lam_init)

            q, k, v = _proj(xs, g_arr, 4 * i + 1, diff_win, j, 0, cfg, [(dw, F32, False)] + outs[1:],
                            tm_s, dw, rope=(*rope_s[LANES // 2], 1))
            o_s = _diff_paged(q.reshape(bs, 1, dw), k.reshape(bs, 1, dw), v.reshape(bs, 1, dw),
                              diff_lambda, j, lam_init, pool_k, pool_v, page_table, DIFF_PAGES_PER_STEP)
            diff_ks.append(k.reshape(bs, ts, 2 * DIFF_HEADS, LANES // 2))
            diff_vs.append(v.reshape(bs, ts, DIFF_HEADS, LANES))
            xs = _out_proj(xs, diff_wo, j, [o_s.reshape(bs, dw)], tm_s, mode="diff", gain=(subln_g, j),
                           gain_scale=1.0 - lam_init)

        xp = _cross_prompt(xp, g_arr, 4 * i + 2, wq_x, wo_x, i, mkv_all, sp, tm_p)
        (q,) = _proj(xs, g_arr, 4 * i + 2, wq_x, i, 0, [(0, 0, (d // X_HEADS) ** -0.5, 0)],
                     [(d, F32, False)], tm_s, d)
        o_s = _cross_sample(_chunk_rows(q, X_HEADS, x_chunks), cache_mem, i)
        xs = _out_proj(xs, wo_x, i, [_unchunk_rows(o_s, X_HEADS, x_chunks)], tm_s)

        final_g = final_norm_g.reshape(1, d) if i == depth - 1 else None
        xp = _ffn(xp, g_arr, 4 * i + 3, wgu, wdn, i, 1, tm_ffn, final_g)
        xs = _ffn(xs, g_arr, 4 * i + 3, wgu, wdn, i, 1, tm_s, final_g)

    yp = xp.reshape(bp, sp, d)
    ys = xs.reshape(bs, ts, d)
    return (yp, ys,
            jnp.stack(mla_p), jnp.stack(mla_s),
            jnp.stack(dil_p[0]), jnp.stack(dil_s[0]),
            jnp.stack(dil_p[1]), jnp.stack(dil_s[1]),
            jnp.stack(dil_p[2]), jnp.stack(dil_s[2]),
            jnp.stack(diff_kp), jnp.stack(diff_ks),
            jnp.stack(diff_vp), jnp.stack(diff_vs),
            mkv_all.reshape(depth, bp, mem_len, 2, X_HEADS, d // X_HEADS))
```
